```python
import math
import jax, jax.numpy as jnp
from jax import lax
import numpy as np

D_MODEL = 4096
BATCH = 4
SEQ = 2048
DEPTH = 4
DEC_BATCH = 32
DEC_SEQ = 4
PAST_LEN = 8192
PAGE_SIZE = 128

N_EVEN = (DEPTH + 1) // 2
N_ODD = DEPTH // 2
N_MOD = 9
EPS = 1e-6
D_FF = 11008
SWA_HEADS = 32
SWA_KV_HEADS = 4
SWA_GROUP = SWA_HEADS // SWA_KV_HEADS
SWA_HEAD_DIM = 64
WINDOW = 128
GDN_HEADS = 16
GDN_DK = 128
GDN_DV = 128
GDN_CHUNK = 64
CONV_W = 4
GDN_CONV_DIM = GDN_HEADS * (2 * GDN_DK + GDN_DV)
SSD_D_INNER = 2 * D_MODEL
SSD_HEAD_DIM = 64
SSD_HEADS = SSD_D_INNER // SSD_HEAD_DIM
SSD_GROUPS = 8
SSD_HPG = SSD_HEADS // SSD_GROUPS
SSD_STATE = 128
SSD_CHUNK = 128
SSD_CONV_DIM = SSD_D_INNER + 2 * SSD_GROUPS * SSD_STATE
EVEN_SPLITS = (SWA_HEADS * SWA_HEAD_DIM, SWA_KV_HEADS * SWA_HEAD_DIM, SWA_KV_HEADS * SWA_HEAD_DIM,
               GDN_CONV_DIM, GDN_HEADS, GDN_HEADS, GDN_HEADS * GDN_DV)
EVEN_IN = sum(EVEN_SPLITS)
EVEN_MIX = SWA_HEADS * SWA_HEAD_DIM + GDN_HEADS * GDN_DV
ODD_SPLITS = (SSD_D_INNER, SSD_CONV_DIM, SSD_HEADS)
ODD_IN = sum(ODD_SPLITS)

kernel_name = 'hybrid_swa_gdn_ssd_macaron_adaln_step'


def _split(x, sizes):
    return jnp.split(x, np.cumsum(sizes)[:-1].tolist(), axis=-1)


def rms_norm(x, w):
    xf = x.astype(jnp.float32)
    return xf * lax.rsqrt(jnp.mean(xf * xf, axis=-1, keepdims=True) + EPS) * w.astype(jnp.float32)


def l2_norm(x):
    xf = x.astype(jnp.float32)
    return xf * lax.rsqrt(jnp.sum(xf * xf, axis=-1, keepdims=True) + EPS)


def ada_norm(x, w, shift, scale):
    return (rms_norm(x, w) * (1.0 + scale) + shift).astype(x.dtype)


def swiglu(h, w_in, w_out):
    gate, up = jnp.split(h @ w_in, 2, axis=-1)
    return (jax.nn.silu(gate) * up) @ w_out


def causal_conv(x, buf, w, b):
    L = x.shape[1]
    xp = jnp.concatenate([buf.astype(x.dtype), x], axis=1)
    y = sum(xp[:, i:i + L] * w[i] for i in range(CONV_W)) + b
    return jax.nn.silu(y), xp[:, xp.shape[1] - (CONV_W - 1):]


def alibi_slopes():
    return jnp.exp2(-8.0 * jnp.arange(1, SWA_HEADS + 1, dtype=jnp.float32) / SWA_HEADS)


def _to_chunks(x, c):
    b, L = x.shape[:2]
    return jnp.moveaxis(x.reshape((b, L // c, c) + x.shape[2:]), 1, 0)


def _from_chunks(y):
    nc, b, c = y.shape[:3]
    return jnp.moveaxis(y, 0, 1).reshape((b, nc * c) + y.shape[3:])


def swa_attend(q, k, v, dist, valid, sinks):
    slopes = alibi_slopes().reshape(SWA_KV_HEADS, SWA_GROUP)[:, :, None, None]
    s = jnp.einsum('bnqhgd,bnkhd->bnhgqk', q.astype(jnp.float32), k.astype(jnp.float32)) * (SWA_HEAD_DIM ** -0.5)
    s = s - slopes * dist[:, None, None]
    s = jnp.where(valid[:, None, None], s, -jnp.inf)
    sink = sinks.astype(jnp.float32).reshape(SWA_KV_HEADS, SWA_GROUP)[:, :, None, None]
    m = jnp.maximum(jnp.max(s, axis=-1, keepdims=True), sink)
    p = jnp.exp(s - m)
    denom = jnp.sum(p, axis=-1, keepdims=True) + jnp.exp(sink - m)
    return jnp.einsum('bnhgqk,bnkhd->bnqhgd', p / denom, v.astype(jnp.float32))


def swa_prompt(q, k, v, sinks):
    b, L = q.shape[:2]
    nb = L // WINDOW
    qb = q.reshape(b, nb, WINDOW, SWA_KV_HEADS, SWA_GROUP, SWA_HEAD_DIM)
    kb = k.reshape(b, nb, WINDOW, SWA_KV_HEADS, SWA_HEAD_DIM)
    vb = v.reshape(b, nb, WINDOW, SWA_KV_HEADS, SWA_HEAD_DIM)
    pad = jnp.zeros_like(kb[:, :1])
    kc = jnp.concatenate([jnp.concatenate([pad, kb[:, :-1]], axis=1), kb], axis=2)
    vc = jnp.concatenate([jnp.concatenate([pad, vb[:, :-1]], axis=1), vb], axis=2)
    i = jnp.arange(WINDOW)[:, None]
    j = jnp.arange(2 * WINDOW)[None, :]
    d = WINDOW + i - j
    blk = jnp.arange(nb)[:, None, None]
    valid = (d >= 0) & (d < WINDOW) & ((blk > 0) | (j >= WINDOW))
    dist = jnp.broadcast_to(d, valid.shape).astype(jnp.float32)
    o = swa_attend(qb, kc, vc, dist, valid, sinks)
    return o.reshape(b, L, SWA_HEADS * SWA_HEAD_DIM)


def swa_sample(q, k, v, kbuf, vbuf, sinks):
    b, L = q.shape[:2]
    wb = kbuf.shape[1]
    kc = jnp.concatenate([kbuf.astype(k.dtype), k], axis=1)
    vc = jnp.concatenate([vbuf.astype(v.dtype), v], axis=1)
    d = wb + jnp.arange(L)[:, None] - jnp.arange(wb + L)[None, :]
    valid = (d >= 0) & (d < WINDOW)
    o = swa_attend(q[:, None], kc[:, None], vc[:, None], d[None].astype(jnp.float32), valid[None], sinks)
    return o.reshape(b, L, SWA_HEADS * SWA_HEAD_DIM), kc[:, L:], vc[:, L:]


def gated_delta_rule(q, k, v, g, beta, s0):
    L = q.shape[1]
    c = min(GDN_CHUNK, L)
    f = lambda t: _to_chunks(t.astype(jnp.float32), c)
    gcum = jnp.cumsum(f(g), axis=2)
    tri_incl = jnp.tril(jnp.ones((c, c), bool))
    tri_strict = jnp.tril(jnp.ones((c, c), bool), -1)
    eye = jnp.eye(c, dtype=jnp.float32)

    def step(S, inp):
        qc, kc, vc, gc, bc = inp
        gt = jnp.swapaxes(gc, 1, 2)
        bt = jnp.swapaxes(bc, 1, 2)
        decay = jnp.exp(jnp.where(tri_incl, gt[..., :, None] - gt[..., None, :], -jnp.inf))
        kk = jnp.einsum('bihd,bjhd->bhij', kc, kc) * bt[..., :, None]
        a = eye + jnp.where(tri_strict, kk * decay, 0.0)
        rhs = jnp.concatenate([kc * (bc * jnp.exp(gc))[..., None], vc * bc[..., None]], axis=-1)
        sol = lax.linalg.triangular_solve(a, jnp.swapaxes(rhs, 1, 2), left_side=True, lower=True,
                                          unit_diagonal=True)
        w, u0 = sol[..., :GDN_DK], sol[..., GDN_DK:]
        u = u0 - jnp.einsum('bhik,bhkv->bhiv', w, S)
        qk = jnp.einsum('bihd,bjhd->bhij', qc, kc) * decay
        o = (jnp.einsum('bihk,bhkv->bihv', qc * jnp.exp(gc)[..., None], S)
             + jnp.einsum('bhij,bhjv->bihv', qk, u))
        g_end = gt[..., -1]
        S = (jnp.exp(g_end)[..., None, None] * S
             + jnp.einsum('bjhk,bhjv->bhkv', kc * jnp.exp(g_end[:, None] - gc)[..., None], u))
        return S, o

    S, o = lax.scan(step, s0.astype(jnp.float32), (f(q), f(k), f(v), gcum, f(beta)))
    return _from_chunks(o), S


def ssd_scan(x, dt, A, Bm, Cm, s0):
    b, L = x.shape[:2]
    c = min(SSD_CHUNK, L)
    f = lambda t: _to_chunks(t.astype(jnp.float32), c)
    dtc = f(dt)
    gcum = jnp.cumsum(dtc * A.astype(jnp.float32), axis=2)
    tri = jnp.tril(jnp.ones((c, c), bool))

    def step(S, inp):
        xc, dc, gc, bc, cc = inp
        xdt = (xc * dc[..., None]).reshape(b, c, SSD_GROUPS, SSD_HPG, SSD_HEAD_DIM)
        g5 = gc.reshape(b, c, SSD_GROUPS, SSD_HPG)
        gt = jnp.moveaxis(g5, 1, 3)
        decay = jnp.exp(jnp.where(tri, gt[..., :, None] - gt[..., None, :], -jnp.inf))
        cb = jnp.einsum('bign,bjgn->bgij', cc, bc)
        y = jnp.einsum('bgij,bgzij,bjgzp->bigzp', cb, decay, xdt)
        Sg = S.reshape(b, SSD_GROUPS, SSD_HPG, SSD_HEAD_DIM, SSD_STATE)
        y = y + jnp.einsum('bign,bgzpn->bigzp', cc, Sg) * jnp.exp(g5)[..., None]
        g_end = g5[:, -1]
        Sg = (jnp.exp(g_end)[..., None, None] * Sg
              + jnp.einsum('bjgn,bjgzp->bgzpn', bc, xdt * jnp.exp(g_end[:, None] - g5)[..., None]))
        return Sg.reshape(b, SSD_HEADS, SSD_HEAD_DIM, SSD_STATE), y.reshape(b, c, SSD_HEADS, SSD_HEAD_DIM)

    S, y = lax.scan(step, s0.astype(jnp.float32), (f(x), dtc, gcum, f(Bm), f(Cm)))
    return _from_chunks(y), S


def even_mixer(h, p, e, kbuf, vbuf, conv_buf, s0):
    b, L, _ = h.shape
    qa, ka, va, qkv_b, a_b, b_b, g_b = _split(h @ p['ev_w_in'][e], EVEN_SPLITS)
    qa = rms_norm(qa.reshape(b, L, SWA_KV_HEADS, SWA_GROUP, SWA_HEAD_DIM), p['swa_q_norm'][e])
    ka = rms_norm(ka.reshape(b, L, SWA_KV_HEADS, SWA_HEAD_DIM), p['swa_k_norm'][e]).astype(h.dtype)
    va = va.reshape(b, L, SWA_KV_HEADS, SWA_HEAD_DIM)
    if kbuf is None:
        o_a = swa_prompt(qa, ka, va, p['swa_sinks'][e])
        wb = min(WINDOW, L)
        new_k, new_v = ka[:, L - wb:], va[:, L - wb:]
    else:
        o_a, new_k, new_v = swa_sample(qa, ka, va, kbuf, vbuf, p['swa_sinks'][e])
    qkv, new_conv = causal_conv(qkv_b, conv_buf, p['gdn_conv_w'][e], p['gdn_conv_b'][e])
    qg, kg, vg = _split(qkv, (GDN_HEADS * GDN_DK, GDN_HEADS * GDN_DK, GDN_HEADS * GDN_DV))
    qg = l2_norm(qg.reshape(b, L, GDN_HEADS, GDN_DK)) * (GDN_DK ** -0.5)
    kg = l2_norm(kg.reshape(b, L, GDN_HEADS, GDN_DK))
    vg = vg.reshape(b, L, GDN_HEADS, GDN_DV)
    g_log = -jnp.exp(p['gdn_a_log'][e]) * jax.nn.softplus(a_b.astype(jnp.float32) + p['gdn_dt_bias'][e])
    beta = jax.nn.sigmoid(b_b.astype(jnp.float32))
    o_b, S = gated_delta_rule(qg, kg, vg, g_log, beta, s0)
    o_b = rms_norm(o_b, p['gdn_norm_w'][e]) * jax.nn.silu(g_b.reshape(b, L, GDN_HEADS, GDN_DV).astype(jnp.float32))
    o = jnp.concatenate([o_a.reshape(b, L, -1), o_b.reshape(b, L, -1)], axis=-1).astype(h.dtype)
    return o @ p['ev_w_out'][e], (new_k, new_v, S, new_conv)


def odd_mixer(h, p, o_idx, conv_buf, s0):
    b, L, _ = h.shape
    z, xbc, dt_raw = _split(h @ p['od_w_in'][o_idx], ODD_SPLITS)
    xbc, new_conv = causal_conv(xbc, conv_buf, p['ssm_conv_w'][o_idx], p['ssm_conv_b'][o_idx])
    xs, Bm, Cm = _split(xbc, (SSD_D_INNER, SSD_GROUPS * SSD_STATE, SSD_GROUPS * SSD_STATE))
    xs = xs.reshape(b, L, SSD_HEADS, SSD_HEAD_DIM)
    Bm = Bm.reshape(b, L, SSD_GROUPS, SSD_STATE)
    Cm = Cm.reshape(b, L, SSD_GROUPS, SSD_STATE)
    dt = jax.nn.softplus(dt_raw.astype(jnp.float32) + p['ssm_dt_bias'][o_idx])
    A = -jnp.exp(p['ssm_a_log'][o_idx].astype(jnp.float32))
    y, S = ssd_scan(xs, dt, A, Bm, Cm, s0)
    y = y + p['ssm_d'][o_idx][:, None] * xs.astype(jnp.float32)
    y = y.reshape(b, L, SSD_D_INNER) * jax.nn.silu(z.astype(jnp.float32))
    y = rms_norm(y.reshape(b, L, SSD_GROUPS, SSD_D_INNER // SSD_GROUPS),
                 p['ssm_norm_w'][o_idx].reshape(SSD_GROUPS, -1)).reshape(b, L, SSD_D_INNER)
    return y.astype(h.dtype) @ p['od_w_out'][o_idx], (S, new_conv)


def trunk(x, c, p, cache):
    bsz = x.shape[0]
    dtx = x.dtype
    out = {'k': [], 'v': [], 'gdn': [], 'gdn_conv': [], 'ssm': [], 'ssm_conv': []}
    for l in range(DEPTH):
        mod = (jax.nn.silu(c) @ p['ada_w'][l] + p['ada_b'][l]).reshape(bsz, N_MOD, 1, D_MODEL)
        h = ada_norm(x, p['norm_w'][l, 0], mod[:, 0], mod[:, 1])
        x = x + (0.5 * mod[:, 2] * swiglu(h, p['ffn_w_in'][l, 0], p['ffn_w_out'][l, 0])).astype(dtx)
        h = ada_norm(x, p['norm_w'][l, 1], mod[:, 3], mod[:, 4])
        if l % 2 == 0:
            e = l // 2
            if cache is None:
                kbuf = vbuf = None
                conv0 = jnp.zeros((bsz, CONV_W - 1, GDN_CONV_DIM), dtx)
                s0 = jnp.zeros((bsz, GDN_HEADS, GDN_DK, GDN_DV), jnp.float32)
            else:
                kbuf, vbuf = cache['k'][e], cache['v'][e]
                conv0, s0 = cache['gdn_conv'][e], cache['gdn'][e]
            o, (nk, nv, ns, ncv) = even_mixer(h, p, e, kbuf, vbuf, conv0, s0)
            out['k'].append(nk)
            out['v'].append(nv)
            out['gdn'].append(ns)
            out['gdn_conv'].append(ncv)
        else:
            oi = l // 2
            if cache is None:
                conv0 = jnp.zeros((bsz, CONV_W - 1, SSD_CONV_DIM), dtx)
                s0 = jnp.zeros((bsz, SSD_HEADS, SSD_HEAD_DIM, SSD_STATE), jnp.float32)
            else:
                conv0, s0 = cache['ssm_conv'][oi], cache['ssm'][oi]
            o, (ns, ncv) = odd_mixer(h, p, oi, conv0, s0)
            out['ssm'].append(ns)
            out['ssm_conv'].append(ncv)
        x = x + (mod[:, 5] * o).astype(dtx)
        h = ada_norm(x, p['norm_w'][l, 2], mod[:, 6], mod[:, 7])
        x = x + (0.5 * mod[:, 8] * swiglu(h, p['ffn_w_in'][l, 1], p['ffn_w_out'][l, 1])).astype(dtx)
    return (x, jnp.stack(out['k']), jnp.stack(out['v']), jnp.stack(out['gdn']), jnp.stack(out['gdn_conv']),
            jnp.stack(out['ssm']), jnp.stack(out['ssm_conv']))


def _dt_bias(k, shape):
    dt = jnp.exp(jax.random.uniform(k, shape, jnp.float32, math.log(1e-3), math.log(1e-1)))
    return dt + jnp.log(-jnp.expm1(-dt))


def _a_log(k, shape):
    return jnp.log(jax.random.uniform(k, shape, jnp.float32, 1.0, 16.0))


def setup_inputs(seed: int = 0) -> dict:
    key = jax.random.key(seed)
    ks = iter(jax.random.split(key, 40))
    nrm = lambda shape, scale: scale * jax.random.normal(next(ks), shape, jnp.float32)
    gain = lambda shape: 1.0 + 0.1 * jax.random.normal(next(ks), shape, jnp.float32)
    wb = min(WINDOW, PAST_LEN)
    return {
        'x_prompt': nrm((BATCH, SEQ, D_MODEL), 1.0),
        'x_sample': nrm((DEC_BATCH, DEC_SEQ, D_MODEL), 1.0),
        'c_prompt': nrm((BATCH, D_MODEL), 1.0),
        'c_sample': nrm((DEC_BATCH, D_MODEL), 1.0),
        'cache_swa_k': nrm((N_EVEN, DEC_BATCH, wb, SWA_KV_HEADS, SWA_HEAD_DIM), 1.0),
        'cache_swa_v': nrm((N_EVEN, DEC_BATCH, wb, SWA_KV_HEADS, SWA_HEAD_DIM), 1.0),
        'state_gdn': nrm((N_EVEN, DEC_BATCH, GDN_HEADS, GDN_DK, GDN_DV), 0.05),
        'state_gdn_conv': nrm((N_EVEN, DEC_BATCH, CONV_W - 1, GDN_CONV_DIM), 1.0),
        'state_ssm': nrm((N_ODD, DEC_BATCH, SSD_HEADS, SSD_HEAD_DIM, SSD_STATE), 0.05),
        'state_ssm_conv': nrm((N_ODD, DEC_BATCH, CONV_W - 1, SSD_CONV_DIM), 1.0),
        'ada_w': nrm((DEPTH, D_MODEL, N_MOD * D_MODEL), 0.5 * D_MODEL ** -0.5),
        'ada_b': nrm((DEPTH, N_MOD * D_MODEL), 0.02),
        'norm_w': gain((DEPTH, 3, D_MODEL)),
        'ffn_w_in': nrm((DEPTH, 2, D_MODEL, 2 * D_FF), D_MODEL ** -0.5),
        'ffn_w_out': nrm((DEPTH, 2, D_FF, D_MODEL), D_FF ** -0.5),
        'ev_w_in': nrm((N_EVEN, D_MODEL, EVEN_IN), D_MODEL ** -0.5),
        'ev_w_out': nrm((N_EVEN, EVEN_MIX, D_MODEL), EVEN_MIX ** -0.5),
        'swa_q_norm': gain((N_EVEN, SWA_HEAD_DIM)),
        'swa_k_norm': gain((N_EVEN, SWA_HEAD_DIM)),
        'swa_sinks': nrm((N_EVEN, SWA_HEADS), 1.0),
        'gdn_conv_w': nrm((N_EVEN, CONV_W, GDN_CONV_DIM), CONV_W ** -0.5),
        'gdn_conv_b': nrm((N_EVEN, GDN_CONV_DIM), 0.02),
        'gdn_a_log': _a_log(next(ks), (N_EVEN, GDN_HEADS)),
        'gdn_dt_bias': _dt_bias(next(ks), (N_EVEN, GDN_HEADS)),
        'gdn_norm_w': gain((N_EVEN, GDN_DV)),
        'od_w_in': nrm((N_ODD, D_MODEL, ODD_IN), D_MODEL ** -0.5),
        'od_w_out': nrm((N_ODD, SSD_D_INNER, D_MODEL), SSD_D_INNER ** -0.5),
        'ssm_conv_w': nrm((N_ODD, CONV_W, SSD_CONV_DIM), CONV_W ** -0.5),
        'ssm_conv_b': nrm((N_ODD, SSD_CONV_DIM), 0.02),
        'ssm_a_log': _a_log(next(ks), (N_ODD, SSD_HEADS)),
        'ssm_dt_bias': _dt_bias(next(ks), (N_ODD, SSD_HEADS)),
        'ssm_d': gain((N_ODD, SSD_HEADS)),
        'ssm_norm_w': gain((N_ODD, SSD_D_INNER)),
    }


def reference(x_prompt, x_sample, c_prompt, c_sample, cache_swa_k, cache_swa_v, state_gdn, state_gdn_conv,
              state_ssm, state_ssm_conv, ada_w, ada_b, norm_w, ffn_w_in, ffn_w_out, ev_w_in, ev_w_out,
              swa_q_norm, swa_k_norm, swa_sinks, gdn_conv_w, gdn_conv_b, gdn_a_log, gdn_dt_bias, gdn_norm_w,
              od_w_in, od_w_out, ssm_conv_w, ssm_conv_b, ssm_a_log, ssm_dt_bias, ssm_d, ssm_norm_w):
    p = {'ada_w': ada_w, 'ada_b': ada_b, 'norm_w': norm_w, 'ffn_w_in': ffn_w_in, 'ffn_w_out': ffn_w_out,
         'ev_w_in': ev_w_in, 'ev_w_out': ev_w_out, 'swa_q_norm': swa_q_norm, 'swa_k_norm': swa_k_norm,
         'swa_sinks': swa_sinks, 'gdn_conv_w': gdn_conv_w, 'gdn_conv_b': gdn_conv_b, 'gdn_a_log': gdn_a_log,
         'gdn_dt_bias': gdn_dt_bias, 'gdn_norm_w': gdn_norm_w, 'od_w_in': od_w_in, 'od_w_out': od_w_out,
         'ssm_conv_w': ssm_conv_w, 'ssm_conv_b': ssm_conv_b, 'ssm_a_log': ssm_a_log,
         'ssm_dt_bias': ssm_dt_bias, 'ssm_d': ssm_d, 'ssm_norm_w': ssm_norm_w}
    y_prompt, k_p, v_p, gdn_p, gdn_conv_p, ssm_p, ssm_conv_p = trunk(x_prompt, c_prompt, p, None)
    cache = {'k': cache_swa_k, 'v': cache_swa_v, 'gdn': state_gdn, 'gdn_conv': state_gdn_conv,
             'ssm': state_ssm, 'ssm_conv': state_ssm_conv}
    y_sample, k_s, v_s, gdn_s, gdn_conv_s, ssm_s, ssm_conv_s = trunk(x_sample, c_sample, p, cache)
    return (y_prompt, y_sample, k_p, v_p, gdn_p, gdn_conv_p, ssm_p, ssm_conv_p,
            k_s, v_s, gdn_s, gdn_conv_s, ssm_s, ssm_conv_s)
```

```python
import functools
import math

import jax
import jax.numpy as jnp
from jax import lax
from jax.experimental import pallas as pl
from jax.experimental.pallas import tpu as pltpu

EPS = 1e-6
N_MOD = 9
SWA_HEAD_DIM = 64
WINDOW = 128
GDN_DK = 128
GDN_DV = 128
GDN_CHUNK = 64
CONV_W = 4
SSD_HEAD_DIM = 64
SSD_STATE = 128
SSD_CHUNK = 128

LANES = 128
SUBLANES = 8
VMEM_LIMIT_BYTES = 56 * 1024 * 1024

MM_DTYPE = jnp.bfloat16
HI = lax.Precision.HIGHEST
F32 = jnp.float32


def _dot(a, b):
    return jnp.dot(a.astype(MM_DTYPE), b.astype(MM_DTYPE), preferred_element_type=F32)


def _dot_nt(a, b):
    return lax.dot_general(a.astype(MM_DTYPE), b.astype(MM_DTYPE), (((1,), (1,)), ((), ())),
                           preferred_element_type=F32)


def _dot_tn(a, b):
    return lax.dot_general(a.astype(MM_DTYPE), b.astype(MM_DTYPE), (((0,), (0,)), ((), ())),
                           preferred_element_type=F32)


def _dot_hi(a, b):
    return jnp.dot(a, b, preferred_element_type=F32, precision=HI)


def _silu(x):
    return x * jax.nn.sigmoid(x)


def _softplus(x):
    return jnp.maximum(x, 0.0) + jnp.log1p(jnp.exp(-jnp.abs(x)))


def _params(*sem):
    return pltpu.CompilerParams(dimension_semantics=sem, vmem_limit_bytes=VMEM_LIMIT_BYTES)


def _tile(n, target):
    if n <= target:
        return n
    t = (target // LANES) * LANES
    while t > LANES and n % t:
        t -= LANES
    assert n % t == 0, (n, target)
    return t


def _ada_kernel(c_ref, w_ref, b_ref, o_ref):
    a = _silu(c_ref[...])
    o_ref[...] = _dot(a, w_ref[...]) + b_ref[...]


def ada_modulation(c_all, ada_w, ada_b):
    depth, d, n = ada_w.shape
    r = c_all.shape[0]
    bn = _tile(n, 512)
    return pl.pallas_call(
        _ada_kernel,
        grid=(depth, n // bn),
        in_specs=[
            pl.BlockSpec((r, d), lambda l, j: (0, 0)),
            pl.BlockSpec((None, d, bn), lambda l, j: (l, 0, j)),
            pl.BlockSpec((None, 1, bn), lambda l, j: (l, 0, j)),
        ],
        out_specs=pl.BlockSpec((None, r, bn), lambda l, j: (l, 0, j)),
        out_shape=jax.ShapeDtypeStruct((depth, r, n), F32),
        compiler_params=_params("parallel", "parallel"),
        name="ada_modulation",
    )(c_all, ada_w, ada_b.reshape(depth, 1, n))


def _ada_norm_kernel(x_ref, w_ref, shift_ref, scale_ref, o_ref):
    x = x_ref[...]
    h = x * lax.rsqrt(jnp.mean(x * x, axis=-1, keepdims=True) + EPS) * w_ref[...]
    o_ref[...] = (h * (1.0 + scale_ref[0]) + shift_ref[0]).astype(o_ref.dtype)


def _mod_spec(mod, rows_per_seq, bm, bn, two_d_grid):
    g, r, _ = mod.shape
    if r == 1:
        assert rows_per_seq % bm == 0
        per = rows_per_seq // bm
        if two_d_grid:
            return pl.BlockSpec((1, 1, bn), lambda i, j: (i // per, 0, j))
        return pl.BlockSpec((1, 1, bn), lambda i: (i // per, 0, 0))
    assert g == 1 and r == bm
    if two_d_grid:
        return pl.BlockSpec((1, r, bn), lambda i, j: (0, 0, j))
    return pl.BlockSpec((1, r, bn), lambda i: (0, 0, 0))


def _row_block(m, target, mod, rows_per_seq):
    return min(m, target, rows_per_seq) if mod.shape[1] == 1 else min(m, target)


def ada_norm(x, w, shift, scale, rows_per_seq):
    m, d = x.shape
    bm = _row_block(m, 256, shift, rows_per_seq)
    return pl.pallas_call(
        _ada_norm_kernel,
        grid=(m // bm,),
        in_specs=[
            pl.BlockSpec((bm, d), lambda i: (i, 0)),
            pl.BlockSpec((1, d), lambda i: (0, 0)),
            _mod_spec(shift, rows_per_seq, bm, d, False),
            _mod_spec(scale, rows_per_seq, bm, d, False),
        ],
        out_specs=pl.BlockSpec((bm, d), lambda i: (i, 0)),
        out_shape=jax.ShapeDtypeStruct((m, d), MM_DTYPE),
        compiler_params=_params("parallel"),
        name="ada_norm",
    )(x, w.reshape(1, d), shift, scale)


def _swiglu_kernel(x_ref, wg_ref, wu_ref, o_ref):
    x = x_ref[...]
    g = jnp.dot(x, wg_ref[...], preferred_element_type=F32)
    u = jnp.dot(x, wu_ref[...], preferred_element_type=F32)
    o_ref[...] = (_silu(g) * u).astype(o_ref.dtype)


def swiglu_up(h, w_in):
    m, d = h.shape
    f = w_in.shape[1] // 2
    bm = min(m, 1024)
    bn = _tile(f, 256)
    nj = f // bn
    return pl.pallas_call(
        _swiglu_kernel,
        grid=(m // bm, nj),
        in_specs=[
            pl.BlockSpec((bm, d), lambda i, j: (i, 0)),
            pl.BlockSpec((d, bn), lambda i, j: (0, j)),
            pl.BlockSpec((d, bn), lambda i, j: (0, j + nj)),
        ],
        out_specs=pl.BlockSpec((bm, bn), lambda i, j: (i, j)),
        out_shape=jax.ShapeDtypeStruct((m, f), MM_DTYPE),
        compiler_params=_params("parallel", "arbitrary"),
        name="swiglu_up",
    )(h, w_in, w_in)


def _mm_resid_kernel(a_ref, w_ref, x_ref, gate_ref, o_ref, *, scale):
    y = jnp.dot(a_ref[...], w_ref[...], preferred_element_type=F32)
    o_ref[...] = x_ref[...] + (scale * gate_ref[0]) * y


def mm_residual(a, w, x, gate, scale, rows_per_seq, bm, bn):
    m, k = a.shape
    n = w.shape[1]
    bm = _row_block(m, bm, gate, rows_per_seq)
    bn = _tile(n, bn)
    return pl.pallas_call(
        functools.partial(_mm_resid_kernel, scale=scale),
        grid=(m // bm, n // bn),
        in_specs=[
            pl.BlockSpec((bm, k), lambda i, j: (i, 0)),
            pl.BlockSpec((k, bn), lambda i, j: (0, j)),
            pl.BlockSpec((bm, bn), lambda i, j: (i, j)),
            _mod_spec(gate, rows_per_seq, bm, bn, True),
        ],
        out_specs=pl.BlockSpec((bm, bn), lambda i, j: (i, j)),
        out_shape=jax.ShapeDtypeStruct((m, n), F32),
        compiler_params=_params("parallel", "arbitrary"),
        name="mm_residual",
    )(a, w, x, gate)


def _mm_kernel(x_ref, w_ref, o_ref):
    o_ref[...] = jnp.dot(x_ref[...], w_ref[...], preferred_element_type=F32)


def mm(x, w, bm, bn):
    m, k = x.shape
    n = w.shape[1]
    bm = min(m, bm)
    bn = _tile(n, bn)
    return pl.pallas_call(
        _mm_kernel,
        grid=(m // bm, n // bn),
        in_specs=[
            pl.BlockSpec((bm, k), lambda i, j: (i, 0)),
            pl.BlockSpec((k, bn), lambda i, j: (0, j)),
        ],
        out_specs=pl.BlockSpec((bm, bn), lambda i, j: (i, j)),
        out_shape=jax.ShapeDtypeStruct((m, n), F32),
        compiler_params=_params("parallel", "arbitrary"),
        name="mm",
    )(x, w)


def _head_norm_kernel(x_ref, w_ref, avg_ref, o_ref):
    x = x_ref[...]
    ms = _dot_hi(x * x, avg_ref[...])
    o_ref[...] = x * lax.rsqrt(ms + EPS) * w_ref[...]


def head_norm(proj, w_cols, ncols):
    m = proj.shape[0]
    bm = min(m, 512)
    lane_head = jnp.arange(LANES) // SWA_HEAD_DIM
    avg = (lane_head[:, None] == lane_head[None, :]).astype(F32) / SWA_HEAD_DIM
    return pl.pallas_call(
        _head_norm_kernel,
        grid=(m // bm, ncols // LANES),
        in_specs=[
            pl.BlockSpec((bm, LANES), lambda i, j: (i, j)),
            pl.BlockSpec((1, LANES), lambda i, j: (0, j)),
            pl.BlockSpec((LANES, LANES), lambda i, j: (0, 0)),
        ],
        out_specs=pl.BlockSpec((bm, LANES), lambda i, j: (i, j)),
        out_shape=jax.ShapeDtypeStruct((m, ncols), F32),
        compiler_params=_params("parallel", "parallel"),
        name="head_norm",
    )(proj, w_cols, avg)


def _swa_kernel(sink_ref, q_ref, kc_ref, kp_ref, vc_ref, vp_ref, o_ref, *, n_heads, n_kv, always_prev):
    group = n_heads // n_kv
    w = WINDOW
    qi = lax.broadcasted_iota(jnp.int32, (w, w), 0)
    kj = lax.broadcasted_iota(jnp.int32, (w, w), 1)
    has_prev = jnp.logical_or(pl.program_id(1) > 0, always_prev)
    valid_p = jnp.logical_and(kj > qi, has_prev)
    valid_c = kj <= qi
    dist_p = (w + qi - kj).astype(F32)
    dist_c = (qi - kj).astype(F32)
    sm_scale = SWA_HEAD_DIM ** -0.5
    neg_inf = jnp.float32(-jnp.inf)
    for kvh in range(n_kv):
        ks = slice(kvh * SWA_HEAD_DIM, (kvh + 1) * SWA_HEAD_DIM)
        kc = kc_ref[0, :, ks]
        kp = kp_ref[0, :, ks]
        vc = vc_ref[0, :, ks]
        vp = vp_ref[0, :, ks]
        for g in range(group):
            h = kvh * group + g
            hs = slice(h * SWA_HEAD_DIM, (h + 1) * SWA_HEAD_DIM)
            slope = 2.0 ** (-8.0 * (h + 1) / n_heads)
            sink = sink_ref[h]
            q = q_ref[0, :, hs]
            s_p = _dot_nt(q, kp) * sm_scale - slope * dist_p
            s_c = _dot_nt(q, kc) * sm_scale - slope * dist_c
            s_p = jnp.where(valid_p, s_p, neg_inf)
            s_c = jnp.where(valid_c, s_c, neg_inf)
            m = jnp.maximum(jnp.maximum(jnp.max(s_p, axis=-1, keepdims=True),
                                        jnp.max(s_c, axis=-1, keepdims=True)), sink)
            p_p = jnp.exp(s_p - m)
            p_c = jnp.exp(s_c - m)
            denom = (jnp.sum(p_p, axis=-1, keepdims=True) + jnp.sum(p_c, axis=-1, keepdims=True)
                     + jnp.exp(sink - m))
            o = (_dot(p_p, vp) + _dot(p_c, vc)) / denom
            o_ref[0, :, hs] = o.astype(o_ref.dtype)


def swa_attention(sinks, q_arr, kc_arr, kp_arr, vc_arr, vp_arr, *, n_heads, n_kv, q_col, kc_col, kp_col,
                  vc_col, vp_col, prev_is_cache):
    b, l, _ = q_arr.shape
    nb = l // WINDOW
    qw = n_heads * SWA_HEAD_DIM
    kw = n_kv * SWA_HEAD_DIM
    if prev_is_cache:
        prev_map = lambda col: (lambda i, n: (i, 0, col))
    else:
        prev_map = lambda col: (lambda i, n: (i, jnp.maximum(n - 1, 0), col))
    return pl.pallas_call(
        functools.partial(_swa_kernel, n_heads=n_heads, n_kv=n_kv, always_prev=prev_is_cache),
        grid=(b, nb),
        in_specs=[
            pl.BlockSpec(memory_space=pltpu.SMEM),
            pl.BlockSpec((1, WINDOW, qw), lambda i, n: (i, n, q_col)),
            pl.BlockSpec((1, WINDOW, kw), lambda i, n: (i, n, kc_col)),
            pl.BlockSpec((1, WINDOW, kw), prev_map(kp_col)),
            pl.BlockSpec((1, WINDOW, kw), lambda i, n: (i, n, vc_col)),
            pl.BlockSpec((1, WINDOW, kw), prev_map(vp_col)),
        ],
        out_specs=pl.BlockSpec((1, WINDOW, qw), lambda i, n: (i, n, 0)),
        out_shape=jax.ShapeDtypeStruct((b, l, qw), MM_DTYPE),
        compiler_params=_params("parallel", "arbitrary"),
        name="swa_attention",
    )(sinks, q_arr, kc_arr, kp_arr, vc_arr, vp_arr)


def _conv_silu_kernel(xp_ref, w_ref, b_ref, o_ref):
    l = o_ref.shape[1]
    y = sum(xp_ref[0, i:i + l, :] * w_ref[i:i + 1, :] for i in range(CONV_W)) + b_ref[...]
    o_ref[0] = _silu(y)


def conv_silu(xp, w, b):
    bsz, lp, c = xp.shape
    l = lp - (CONV_W - 1)
    bc = _tile(c, 512)
    return pl.pallas_call(
        _conv_silu_kernel,
        grid=(bsz, c // bc),
        in_specs=[
            pl.BlockSpec((1, lp, bc), lambda i, j: (i, 0, j)),
            pl.BlockSpec((CONV_W, bc), lambda i, j: (0, j)),
            pl.BlockSpec((1, bc), lambda i, j: (0, j)),
        ],
        out_specs=pl.BlockSpec((1, l, bc), lambda i, j: (i, 0, j)),
        out_shape=jax.ShapeDtypeStruct((bsz, l, c), F32),
        compiler_params=_params("parallel", "parallel"),
        name="conv_silu",
    )(xp, w, b.reshape(1, c))


def _unit_lower_inverse(a_strict, c):
    ri = lax.broadcasted_iota(jnp.int32, (c, c), 0)
    ci = lax.broadcasted_iota(jnp.int32, (c, c), 1)
    eye = (ri == ci).astype(F32)

    def same_block(bits):
        return lax.shift_right_logical(ri, bits) == lax.shift_right_logical(ci, bits)

    n1 = jnp.where(same_block(3), a_strict, 0.0)
    n2 = _dot_hi(n1, n1)
    n4 = _dot_hi(n2, n2)
    t = _dot_hi(_dot_hi(eye - n1, eye + n2), eye + n4)
    bits = 3
    while (1 << bits) < c:
        off = jnp.where(jnp.logical_and(same_block(bits + 1), jnp.logical_not(same_block(bits))), a_strict, 0.0)
        t = t - _dot_hi(_dot_hi(t, off), t)
        bits += 1
    return t


def _gdn_kernel(alog_ref, dtb_ref, q_ref, k_ref, v_ref, ab_ref, gate_ref, nw_ref, s0_ref, o_ref, s_ref,
                *, c, n_heads, n_valid):
    h = pl.program_id(1)
    lp = q_ref.shape[1]
    nc = lp // c
    neg_a = -jnp.exp(alog_ref[h])
    dt_bias = dtb_ref[h]
    ri = lax.broadcasted_iota(jnp.int32, (c, c), 0)
    ci = lax.broadcasted_iota(jnp.int32, (c, c), 1)
    eye = ri == ci
    lower = ri >= ci
    strict = ri > ci
    lane = lax.broadcasted_iota(jnp.int32, (c, LANES), 1)
    row = lax.broadcasted_iota(jnp.int32, (c, 1), 0)
    s_ref[0, 0] = s0_ref[0, 0]

    def chunk(t, carry):
        rows = pl.ds(pl.multiple_of(t * c, c), c)
        q = q_ref[0, rows, :]
        k = k_ref[0, rows, :]
        v = v_ref[0, rows, :]
        q = q * lax.rsqrt(jnp.sum(q * q, axis=-1, keepdims=True) + EPS) * (GDN_DK ** -0.5)
        k = k * lax.rsqrt(jnp.sum(k * k, axis=-1, keepdims=True) + EPS)
        ab = ab_ref[0, rows, :]
        a_col = jnp.sum(jnp.where(lane == h, ab, 0.0), axis=-1, keepdims=True)
        b_col = jnp.sum(jnp.where(lane == h + n_heads, ab, 0.0), axis=-1, keepdims=True)
        g_col = neg_a * _softplus(a_col + dt_bias)
        beta = jax.nn.sigmoid(b_col)
        if n_valid < lp:
            live = (row + t * c) < n_valid
            g_col = jnp.where(live, g_col, 0.0)
            beta = jnp.where(live, beta, 0.0)
        g_bc = jnp.broadcast_to(g_col, (c, c))
        g_row = jnp.sum(jnp.where(eye, g_bc, 0.0), axis=0, keepdims=True)
        gc_row = jnp.sum(jnp.where(ri <= ci, g_bc, 0.0), axis=0, keepdims=True)
        gc_col = jnp.sum(jnp.where(lower, jnp.broadcast_to(g_row, (c, c)), 0.0), axis=1, keepdims=True)
        decay = jnp.where(lower, jnp.exp(gc_col - gc_row), 0.0)
        kk = _dot_nt(k, k) * beta
        t_inv = _unit_lower_inverse(jnp.where(strict, kk * decay, 0.0), c)
        s = s_ref[0, 0]
        w = _dot_hi(t_inv, k * (beta * jnp.exp(gc_col)))
        u0 = _dot_hi(t_inv, v * beta)
        u = u0 - _dot(w, s)
        qk = _dot_nt(q, k) * decay
        o = _dot(q * jnp.exp(gc_col), s) + _dot(qk, u)
        g_end = gc_col[c - 1:c, :]
        s_ref[0, 0] = jnp.exp(g_end) * s + _dot_tn(k * jnp.exp(g_end - gc_col), u)
        o = o * lax.rsqrt(jnp.mean(o * o, axis=-1, keepdims=True) + EPS) * nw_ref[...]
        o_ref[0, rows, :] = (o * _silu(gate_ref[0, rows, :])).astype(o_ref.dtype)
        return carry

    lax.fori_loop(0, nc, chunk, 0)


def gated_deltanet(a_log, dt_bias, qkv, ab_arr, gate_arr, norm_w, s0, *, ab_col, gate_col, n_valid):
    bsz, lp, _ = qkv.shape
    n_heads = s0.shape[1]
    c = min(GDN_CHUNK, lp)
    blk = lambda col0: pl.BlockSpec((1, lp, LANES), lambda i, h: (i, 0, col0 + h))
    return pl.pallas_call(
        functools.partial(_gdn_kernel, c=c, n_heads=n_heads, n_valid=n_valid),
        grid=(bsz, n_heads),
        in_specs=[
            pl.BlockSpec(memory_space=pltpu.SMEM),
            pl.BlockSpec(memory_space=pltpu.SMEM),
            blk(0), blk(n_heads), blk(2 * n_heads),
            pl.BlockSpec((1, lp, LANES), lambda i, h: (i, 0, ab_col)),
            blk(gate_col),
            pl.BlockSpec((1, GDN_DV), lambda i, h: (0, 0)),
            pl.BlockSpec((1, 1, GDN_DK, GDN_DV), lambda i, h: (i, h, 0, 0)),
        ],
        out_specs=[
            pl.BlockSpec((1, lp, GDN_DV), lambda i, h: (i, 0, h)),
            pl.BlockSpec((1, 1, GDN_DK, GDN_DV), lambda i, h: (i, h, 0, 0)),
        ],
        out_shape=[
            jax.ShapeDtypeStruct((bsz, lp, n_heads * GDN_DV), MM_DTYPE),
            jax.ShapeDtypeStruct(s0.shape, F32),
        ],
        compiler_params=_params("parallel", "parallel"),
        name="gated_deltanet",
    )(a_log, dt_bias, qkv, qkv, qkv, ab_arr, gate_arr, norm_w.reshape(1, GDN_DV), s0)


def _ssd_kernel(x_ref, b_ref, c_ref, z_ref, dt_ref, alog_ref, dtb_ref, d_ref, nw_ref, s0_ref, y_ref, s_ref,
                xw_scr, y_scr, *, hpg, lp, n_valid):
    g = pl.program_id(1)
    t = pl.program_id(2)
    c = x_ref.shape[1]
    p = SSD_HEAD_DIM

    @pl.when(t == 0)
    def _():
        s_ref[0, 0] = s0_ref[0, 0]

    ri = lax.broadcasted_iota(jnp.int32, (c, c), 0)
    ci = lax.broadcasted_iota(jnp.int32, (c, c), 1)
    lower = ri >= ci
    tri = lower.astype(F32)
    hl = lax.broadcasted_iota(jnp.int32, (LANES, LANES), 0)
    zl = lax.broadcasted_iota(jnp.int32, (LANES, LANES), 1)
    sel = jnp.logical_and(hl == g * hpg + zl, zl < hpg).astype(F32)

    xs = x_ref[0]
    bm = b_ref[0]
    cm = c_ref[0]
    dt = _softplus(dt_ref[0] + dtb_ref[...])
    if n_valid < lp:
        live = (lax.broadcasted_iota(jnp.int32, (c, 1), 0) + t * c) < n_valid
        dt = jnp.where(live, dt, 0.0)
    gcum = _dot_hi(tri, dt * (-jnp.exp(alog_ref[...])))
    dtg = _dot_hi(dt, sel)
    gcg = _dot_hi(gcum, sel)
    gcg_t = gcg.T
    d_g = _dot_hi(jnp.broadcast_to(d_ref[...], (SUBLANES, LANES)), sel)[0:1, :]
    cb = _dot_nt(cm, bm)
    s = s_ref[0, 0]
    y_state = _dot(cm, s)
    for z in range(hpg):
        sl = slice(z * p, (z + 1) * p)
        g_col = gcg[:, z:z + 1]
        g_row = gcg_t[z:z + 1, :]
        x_z = xs[:, sl]
        xdt = x_z * dtg[:, z:z + 1]
        decay = jnp.where(lower, jnp.exp(g_col - g_row), 0.0)
        y_z = _dot(cb * decay, xdt) + y_state[:, sl] * jnp.exp(g_col)
        y_scr[:, sl] = y_z + d_g[:, z:z + 1] * x_z
        g_end = g_col[c - 1:c, :]
        xw_scr[:, sl] = xdt * jnp.exp(g_end - g_col)
        s_ref[0, 0, :, sl] = jnp.exp(g_end) * s[:, sl]
    s_ref[0, 0] = s_ref[0, 0] + _dot_tn(bm, xw_scr[...])
    y = y_scr[...] * _silu(z_ref[0])
    y = y * lax.rsqrt(jnp.mean(y * y, axis=-1, keepdims=True) + EPS) * nw_ref[...]
    y_ref[0] = y.astype(y_ref.dtype)


def ssd_mixer(xbc, z_arr, dt_arr, a_log, dt_bias, d_skip, norm_w, s0_t, *, n_groups, z_col, dt_col, n_valid):
    bsz, lp, _ = xbc.shape
    n_heads = a_log.shape[0]
    assert n_heads == LANES
    hpg = n_heads // n_groups
    gw = hpg * SSD_HEAD_DIM
    d_inner = n_heads * SSD_HEAD_DIM
    c = min(SSD_CHUNK, lp)
    nc = lp // c
    assert gw % LANES == 0 and d_inner % gw == 0
    b_col0 = d_inner // SSD_STATE
    vec = lambda a: a.reshape(1, n_heads)
    return pl.pallas_call(
        functools.partial(_ssd_kernel, hpg=hpg, lp=lp, n_valid=n_valid),
        grid=(bsz, n_groups, nc),
        in_specs=[
            pl.BlockSpec((1, c, gw), lambda i, g, t: (i, t, g)),
            pl.BlockSpec((1, c, SSD_STATE), lambda i, g, t: (i, t, b_col0 + g)),
            pl.BlockSpec((1, c, SSD_STATE), lambda i, g, t: (i, t, b_col0 + n_groups + g)),
            pl.BlockSpec((1, c, gw), lambda i, g, t: (i, t, z_col + g)),
            pl.BlockSpec((1, c, LANES), lambda i, g, t: (i, t, dt_col)),
            pl.BlockSpec((1, LANES), lambda i, g, t: (0, 0)),
            pl.BlockSpec((1, LANES), lambda i, g, t: (0, 0)),
            pl.BlockSpec((1, LANES), lambda i, g, t: (0, 0)),
            pl.BlockSpec((1, gw), lambda i, g, t: (0, g)),
            pl.BlockSpec((1, 1, SSD_STATE, gw), lambda i, g, t: (i, g, 0, 0)),
        ],
        out_specs=[
            pl.BlockSpec((1, c, gw), lambda i, g, t: (i, t, g)),
            pl.BlockSpec((1, 1, SSD_STATE, gw), lambda i, g, t: (i, g, 0, 0)),
        ],
        out_shape=[
            jax.ShapeDtypeStruct((bsz, lp, d_inner), MM_DTYPE),
            jax.ShapeDtypeStruct(s0_t.shape, F32),
        ],
        scratch_shapes=[pltpu.VMEM((c, gw), F32), pltpu.VMEM((c, gw), F32)],
        compiler_params=_params("parallel", "parallel", "arbitrary"),
        name="ssd_mixer",
    )(xbc, xbc, xbc, z_arr, dt_arr, vec(a_log), vec(dt_bias), vec(d_skip), norm_w.reshape(1, d_inner), s0_t)


def _pad_rows(a, lp):
    return jnp.pad(a, ((0, 0), (0, lp - a.shape[1]), (0, 0)))


def _even_weights(ev_w_in, n_heads_a, n_kv, n_heads_b):
    qa = n_heads_a * SWA_HEAD_DIM
    kv = n_kv * SWA_HEAD_DIM
    conv = n_heads_b * (2 * GDN_DK + GDN_DV)
    o_ab = qa + 2 * kv + conv
    o_g = o_ab + 2 * n_heads_b
    main = ev_w_in[:, :, :o_ab]
    ab = ev_w_in[:, :, o_ab:o_g]
    gate = ev_w_in[:, :, o_g:]
    pad = jnp.zeros(ab.shape[:2] + (LANES - 2 * n_heads_b,), ev_w_in.dtype)
    return jnp.concatenate([main, gate, ab, pad], axis=-1).astype(MM_DTYPE)


def _trunk(x3, mod_all, p, cache):
    bsz, l, d = x3.shape
    m = bsz * l
    x = x3.reshape(m, d)
    depth = p['norm_w'].shape[0]
    n_heads_a = p['swa_sinks'].shape[1]
    n_kv = p['n_kv']
    n_heads_b = p['gdn_a_log'].shape[1]
    n_heads_c = p['ssm_a_log'].shape[1]
    d_inner = n_heads_c * SSD_HEAD_DIM
    n_groups = (p['ssm_conv_w'].shape[2] - d_inner) // (2 * SSD_STATE)
    hpg = n_heads_c // n_groups
    qa = n_heads_a * SWA_HEAD_DIM
    kv = n_kv * SWA_HEAD_DIM
    conv_b = n_heads_b * (2 * GDN_DK + GDN_DV)
    conv_c = d_inner + 2 * n_groups * SSD_STATE
    expand = cache is not None
    out = {'k': [], 'v': [], 'gdn': [], 'gdn_conv': [], 'ssm': [], 'ssm_conv': []}

    def mod_of(layer, idx):
        v = mod_all[layer, :, idx]
        if expand:
            return jnp.repeat(v, l, axis=0).reshape(1, m, d)
        return v.reshape(bsz, 1, d)

    for layer in range(depth):
        mo = functools.partial(mod_of, layer)
        h = ada_norm(x, p['norm_w'][layer, 0], mo(0), mo(1), l)
        act = swiglu_up(h, p['ffn_w_in'][layer, 0])
        x = mm_residual(act, p['ffn_w_out'][layer, 0], x, mo(2), 0.5, l, 512, 512)
        h = ada_norm(x, p['norm_w'][layer, 1], mo(3), mo(4), l)
        if layer % 2 == 0:
            e = layer // 2
            proj = mm(h, p['ev_w_in'][e], 1024, 640)
            ncols = proj.shape[1]
            proj3 = proj.reshape(bsz, l, ncols)
            w_cols = jnp.concatenate([jnp.tile(p['swa_q_norm'][e], n_heads_a),
                                      jnp.tile(p['swa_k_norm'][e], n_kv)]).reshape(1, qa + kv)
            qk = head_norm(proj, w_cols, qa + kv).reshape(bsz, l, qa + kv)
            k_new = qk[:, :, qa:]
            v_new = proj3[:, :, qa + kv:qa + 2 * kv]
            if cache is None:
                o_a = swa_attention(p['swa_sinks'][e], qk, qk, qk, proj3, proj3, n_heads=n_heads_a, n_kv=n_kv,
                                    q_col=0, kc_col=qa // kv, kp_col=qa // kv, vc_col=(qa + kv) // kv,
                                    vp_col=(qa + kv) // kv, prev_is_cache=False)
                wb = min(WINDOW, l)
                out['k'].append(k_new[:, l - wb:].reshape(bsz, wb, n_kv, SWA_HEAD_DIM))
                out['v'].append(v_new[:, l - wb:].reshape(bsz, wb, n_kv, SWA_HEAD_DIM))
                conv0 = jnp.zeros((bsz, CONV_W - 1, conv_b), F32)
                s0 = jnp.zeros((bsz, n_heads_b, GDN_DK, GDN_DV), F32)
            else:
                kbuf = cache['k'][e].reshape(bsz, -1, kv)
                vbuf = cache['v'][e].reshape(bsz, -1, kv)
                o_a = swa_attention(p['swa_sinks'][e], _pad_rows(qk, WINDOW), _pad_rows(qk, WINDOW), kbuf,
                                    _pad_rows(v_new, WINDOW), vbuf, n_heads=n_heads_a, n_kv=n_kv,
                                    q_col=0, kc_col=qa // kv, kp_col=0, vc_col=0, vp_col=0,
                                    prev_is_cache=True)[:, :l]
                out['k'].append(jnp.concatenate([kbuf, k_new], axis=1)[:, l:].reshape(bsz, -1, n_kv, SWA_HEAD_DIM))
                out['v'].append(jnp.concatenate([vbuf, v_new], axis=1)[:, l:].reshape(bsz, -1, n_kv, SWA_HEAD_DIM))
                conv0 = cache['gdn_conv'][e]
                s0 = cache['gdn'][e]
            o_qkv = qa + 2 * kv
            xp = jnp.concatenate([conv0, proj3[:, :, o_qkv:o_qkv + conv_b]], axis=1)
            out['gdn_conv'].append(xp[:, xp.shape[1] - (CONV_W - 1):])
            qkv = conv_silu(xp, p['gdn_conv_w'][e], p['gdn_conv_b'][e])
            o_gate = o_qkv + conv_b
            o_ab = o_gate + n_heads_b * GDN_DV
            if l % GDN_CHUNK == 0:
                lp = l
                ab_arr, gate_arr = proj3, proj3
                ab_col, gate_col = o_ab // LANES, o_gate // LANES
            else:
                lp = -(-l // GDN_CHUNK) * GDN_CHUNK
                qkv = _pad_rows(qkv, lp)
                ab_arr = _pad_rows(proj3[:, :, o_ab:o_ab + LANES], lp)
                gate_arr = _pad_rows(proj3[:, :, o_gate:o_ab], lp)
                ab_col, gate_col = 0, 0
            o_b, s_new = gated_deltanet(p['gdn_a_log'][e], p['gdn_dt_bias'][e], qkv, ab_arr, gate_arr,
                                        p['gdn_norm_w'][e], s0, ab_col=ab_col, gate_col=gate_col, n_valid=l)
            out['gdn'].append(s_new)
            o = jnp.concatenate([o_a, o_b[:, :l]], axis=-1).reshape(m, -1)
            x = mm_residual(o, p['ev_w_out'][e], x, mo(5), 1.0, l, 1024, 512)
        else:
            oi = layer // 2
            proj = mm(h, p['od_w_in'][oi], 1024, 640)
            proj3 = proj.reshape(bsz, l, -1)
            if cache is None:
                conv0 = jnp.zeros((bsz, CONV_W - 1, conv_c), F32)
                s0_t = jnp.zeros((bsz, n_groups, SSD_STATE, hpg * SSD_HEAD_DIM), F32)
            else:
                conv0 = cache['ssm_conv'][oi]
                s0_t = (cache['ssm'][oi].reshape(bsz, n_groups, hpg * SSD_HEAD_DIM, SSD_STATE)
                        .transpose(0, 1, 3, 2))
            xp = jnp.concatenate([conv0, proj3[:, :, d_inner:d_inner + conv_c]], axis=1)
            out['ssm_conv'].append(xp[:, xp.shape[1] - (CONV_W - 1):])
            xbc = conv_silu(xp, p['ssm_conv_w'][oi], p['ssm_conv_b'][oi])
            gw = hpg * SSD_HEAD_DIM
            o_dt = d_inner + conv_c
            if l % SSD_CHUNK == 0:
                z_arr, dt_arr = proj3, proj3
                z_col, dt_col = 0, o_dt // LANES
            else:
                lp = -(-l // SSD_CHUNK) * SSD_CHUNK
                xbc = _pad_rows(xbc, lp)
                z_arr = _pad_rows(proj3[:, :, :d_inner], lp)
                dt_arr = _pad_rows(proj3[:, :, o_dt:], lp)
                z_col, dt_col = 0, 0
            y, s_t = ssd_mixer(xbc, z_arr, dt_arr, p['ssm_a_log'][oi], p['ssm_dt_bias'][oi], p['ssm_d'][oi],
                               p['ssm_norm_w'][oi], s0_t, n_groups=n_groups, z_col=z_col, dt_col=dt_col,
                               n_valid=l)
            out['ssm'].append(s_t.transpose(0, 1, 3, 2).reshape(bsz, n_heads_c, SSD_HEAD_DIM, SSD_STATE))
            x = mm_residual(y[:, :l].reshape(m, d_inner), p['od_w_out'][oi], x, mo(5), 1.0, l, 512, 512)
        h = ada_norm(x, p['norm_w'][layer, 2], mo(6), mo(7), l)
        act = swiglu_up(h, p['ffn_w_in'][layer, 1])
        x = mm_residual(act, p['ffn_w_out'][layer, 1], x, mo(8), 0.5, l, 512, 512)
    return (x.reshape(bsz, l, d), jnp.stack(out['k']), jnp.stack(out['v']), jnp.stack(out['gdn']),
            jnp.stack(out['gdn_conv']), jnp.stack(out['ssm']), jnp.stack(out['ssm_conv']))


def kernel(x_prompt, x_sample, c_prompt, c_sample, cache_swa_k, cache_swa_v, state_gdn, state_gdn_conv, state_ssm, state_ssm_conv, ada_w, ada_b, norm_w, ffn_w_in, ffn_w_out, ev_w_in, ev_w_out, swa_q_norm, swa_k_norm, swa_sinks, gdn_conv_w, gdn_conv_b, gdn_a_log, gdn_dt_bias, gdn_norm_w, od_w_in, od_w_out, ssm_conv_w, ssm_conv_b, ssm_a_log, ssm_dt_bias, ssm_d, ssm_norm_w):
    depth, d = norm_w.shape[0], norm_w.shape[2]
    n_kv = cache_swa_k.shape[3]
    n_heads_a = swa_sinks.shape[1]
    n_heads_b = gdn_a_log.shape[1]
    bp, bs = c_prompt.shape[0], c_sample.shape[0]
    r = bp + bs
    r_pad = -(-r // 16) * 16
    c_all = jnp.pad(jnp.concatenate([c_prompt, c_sample], axis=0), ((0, r_pad - r), (0, 0)))
    mod = ada_modulation(c_all, ada_w, ada_b).reshape(depth, r_pad, N_MOD, d)
    p = {
        'norm_w': norm_w,
        'ffn_w_in': ffn_w_in.astype(MM_DTYPE), 'ffn_w_out': ffn_w_out.astype(MM_DTYPE),
        'ev_w_in': _even_weights(ev_w_in, n_heads_a, n_kv, n_heads_b), 'ev_w_out': ev_w_out.astype(MM_DTYPE),
        'swa_q_norm': swa_q_norm, 'swa_k_norm': swa_k_norm, 'swa_sinks': swa_sinks, 'n_kv': n_kv,
        'gdn_conv_w': gdn_conv_w, 'gdn_conv_b': gdn_conv_b, 'gdn_a_log': gdn_a_log,
        'gdn_dt_bias': gdn_dt_bias, 'gdn_norm_w': gdn_norm_w,
        'od_w_in': od_w_in.astype(MM_DTYPE), 'od_w_out': od_w_out.astype(MM_DTYPE),
        'ssm_conv_w': ssm_conv_w, 'ssm_conv_b': ssm_conv_b, 'ssm_a_log': ssm_a_log,
        'ssm_dt_bias': ssm_dt_bias, 'ssm_d': ssm_d, 'ssm_norm_w': ssm_norm_w,
    }
    cache = {'k': cache_swa_k, 'v': cache_swa_v, 'gdn': state_gdn, 'gdn_conv': state_gdn_conv,
             'ssm': state_ssm, 'ssm_conv': state_ssm_conv}
    y_p, k_p, v_p, gdn_p, gdn_conv_p, ssm_p, ssm_conv_p = _trunk(x_prompt, mod[:, :bp], p, None)
    y_s, k_s, v_s, gdn_s, gdn_conv_s, ssm_s, ssm_conv_s = _trunk(x_sample, mod[:, bp:r], p, cache)
    return (y_p, y_s, k_p, v_p, gdn_p, gdn_conv_p, ssm_p, ssm_conv_p,
            k_s, v_s, gdn_s, gdn_conv_s, ssm_s, ssm_conv_s)
```

```python
import functools
import math

import jax
import jax.numpy as jnp
from jax import lax
from jax.experimental import pallas as pl
from jax.experimental.pallas import tpu as pltpu

EPS = 1e-6
N_MOD = 9
SWA_HEAD_DIM = 64
WINDOW = 128
GDN_DK = 128
GDN_DV = 128
GDN_CHUNK = 64
CONV_W = 4
SSD_HEAD_DIM = 64
SSD_STATE = 128
SSD_CHUNK = 128

LANES = 128
SUBLANES = 8
VMEM_LIMIT_BYTES = 56 * 1024 * 1024

MM_DTYPE = jnp.bfloat16
HI = lax.Precision.HIGHEST
F32 = jnp.float32


def _dot(a, b):
    return jnp.dot(a.astype(MM_DTYPE), b.astype(MM_DTYPE), preferred_element_type=F32)


def _dot_nt(a, b):
    return lax.dot_general(a.astype(MM_DTYPE), b.astype(MM_DTYPE), (((1,), (1,)), ((), ())),
                           preferred_element_type=F32)


def _dot_tn(a, b):
    return lax.dot_general(a.astype(MM_DTYPE), b.astype(MM_DTYPE), (((0,), (0,)), ((), ())),
                           preferred_element_type=F32)


def _dot_hi(a, b):
    return jnp.dot(a, b, preferred_element_type=F32, precision=HI)


def _hi_lo(a):
    hi = a.astype(MM_DTYPE)
    return hi, (a - hi.astype(F32)).astype(MM_DTYPE)


def _dot3(a, b):
    ah, al = _hi_lo(a)
    bh, bl = _hi_lo(b)
    return (jnp.dot(ah, bh, preferred_element_type=F32)
            + (jnp.dot(ah, bl, preferred_element_type=F32) + jnp.dot(al, bh, preferred_element_type=F32)))


def _silu(x):
    return x * jax.nn.sigmoid(x)


def _softplus(x):
    return jnp.maximum(x, 0.0) + jnp.log1p(jnp.exp(-jnp.abs(x)))


def _params(*sem):
    return pltpu.CompilerParams(dimension_semantics=sem, vmem_limit_bytes=VMEM_LIMIT_BYTES)


def _tile(n, target):
    if n <= target:
        return n
    t = (target // LANES) * LANES
    while t > LANES and n % t:
        t -= LANES
    assert n % t == 0, (n, target)
    return t


def _ada_kernel(c_ref, w_ref, b_ref, o_ref):
    a = _silu(c_ref[...])
    o_ref[...] = _dot(a, w_ref[...]) + b_ref[...]


def ada_modulation(c_all, ada_w, ada_b):
    depth, d, n = ada_w.shape
    r = c_all.shape[0]
    bn = _tile(n, 512)
    return pl.pallas_call(
        _ada_kernel,
        grid=(depth, n // bn),
        in_specs=[
            pl.BlockSpec((r, d), lambda l, j: (0, 0)),
            pl.BlockSpec((None, d, bn), lambda l, j: (l, 0, j)),
            pl.BlockSpec((None, 1, bn), lambda l, j: (l, 0, j)),
        ],
        out_specs=pl.BlockSpec((None, r, bn), lambda l, j: (l, 0, j)),
        out_shape=jax.ShapeDtypeStruct((depth, r, n), F32),
        compiler_params=_params("parallel", "parallel"),
        name="ada_modulation",
    )(c_all, ada_w, ada_b.reshape(depth, 1, n))


def _ada_norm_kernel(x_ref, w_ref, shift_ref, scale_ref, o_ref):
    x = x_ref[...]
    h = x * lax.rsqrt(jnp.mean(x * x, axis=-1, keepdims=True) + EPS) * w_ref[...]
    o_ref[...] = (h * (1.0 + scale_ref[0]) + shift_ref[0]).astype(o_ref.dtype)


def _mod_spec(mod, rows_per_seq, bm, bn, two_d_grid):
    g, r, _ = mod.shape
    if r == 1:
        assert rows_per_seq % bm == 0
        per = rows_per_seq // bm
        if two_d_grid:
            return pl.BlockSpec((1, 1, bn), lambda i, j: (i // per, 0, j))
        return pl.BlockSpec((1, 1, bn), lambda i: (i // per, 0, 0))
    assert g == 1 and r == bm
    if two_d_grid:
        return pl.BlockSpec((1, r, bn), lambda i, j: (0, 0, j))
    return pl.BlockSpec((1, r, bn), lambda i: (0, 0, 0))


def _row_block(m, target, mod, rows_per_seq):
    return min(m, target, rows_per_seq) if mod.shape[1] == 1 else min(m, target)


def ada_norm(x, w, shift, scale, rows_per_seq):
    m, d = x.shape
    bm = _row_block(m, 256, shift, rows_per_seq)
    return pl.pallas_call(
        _ada_norm_kernel,
        grid=(m // bm,),
        in_specs=[
            pl.BlockSpec((bm, d), lambda i: (i, 0)),
            pl.BlockSpec((1, d), lambda i: (0, 0)),
            _mod_spec(shift, rows_per_seq, bm, d, False),
            _mod_spec(scale, rows_per_seq, bm, d, False),
        ],
        out_specs=pl.BlockSpec((bm, d), lambda i: (i, 0)),
        out_shape=jax.ShapeDtypeStruct((m, d), MM_DTYPE),
        compiler_params=_params("parallel"),
        name="ada_norm",
    )(x, w.reshape(1, d), shift, scale)


def _swiglu_kernel(x_ref, wg_ref, wu_ref, o_ref):
    x = x_ref[...]
    g = jnp.dot(x, wg_ref[...].astype(x.dtype), preferred_element_type=F32)
    u = jnp.dot(x, wu_ref[...].astype(x.dtype), preferred_element_type=F32)
    o_ref[...] = (_silu(g) * u).astype(o_ref.dtype)


def swiglu_up(h, w_in, idx):
    m, d = h.shape
    f = w_in.shape[2] // 2
    bm = min(m, 1024)
    bn = _tile(f, 256)
    nj = f // bn
    return pl.pallas_call(
        _swiglu_kernel,
        grid=(m // bm, nj),
        in_specs=[
            pl.BlockSpec((bm, d), lambda i, j: (i, 0)),
            pl.BlockSpec((None, d, bn), lambda i, j: (idx, 0, j)),
            pl.BlockSpec((None, d, bn), lambda i, j: (idx, 0, j + nj)),
        ],
        out_specs=pl.BlockSpec((bm, bn), lambda i, j: (i, j)),
        out_shape=jax.ShapeDtypeStruct((m, f), MM_DTYPE),
        compiler_params=_params("parallel", "arbitrary"),
        name="swiglu_up",
    )(h, w_in, w_in)


def _mm_resid_kernel(*refs, n_lhs, scale):
    a_refs, w_refs = refs[:n_lhs], refs[n_lhs:2 * n_lhs]
    x_ref, gate_ref, o_ref = refs[2 * n_lhs:]
    y = None
    for a_ref, w_ref in zip(a_refs, w_refs):
        a = a_ref[...]
        d = jnp.dot(a, w_ref[...].astype(a.dtype), preferred_element_type=F32)
        y = d if y is None else y + d
    o_ref[...] = x_ref[...] + (scale * gate_ref[0]) * y


def mm_residual(a_list, w, idx, x, gate, scale, rows_per_seq, bm, bn):
    m, k = a_list[0].shape
    n_lhs = len(a_list)
    n = w.shape[2]
    assert w.shape[1] == n_lhs * k
    bm = _row_block(m, bm, gate, rows_per_seq)
    bn = _tile(n, bn)
    w_spec = lambda part: pl.BlockSpec((None, k, bn), lambda i, j: (idx, part, j))
    return pl.pallas_call(
        functools.partial(_mm_resid_kernel, n_lhs=n_lhs, scale=scale),
        grid=(m // bm, n // bn),
        in_specs=(
            [pl.BlockSpec((bm, k), lambda i, j: (i, 0)) for _ in range(n_lhs)]
            + [w_spec(part) for part in range(n_lhs)]
            + [pl.BlockSpec((bm, bn), lambda i, j: (i, j)), _mod_spec(gate, rows_per_seq, bm, bn, True)]
        ),
        out_specs=pl.BlockSpec((bm, bn), lambda i, j: (i, j)),
        out_shape=jax.ShapeDtypeStruct((m, n), F32),
        compiler_params=_params("parallel", "arbitrary"),
        name="mm_residual",
    )(*a_list, *([w] * n_lhs), x, gate)


def _mm_kernel(x_ref, w_ref, o_ref):
    x = x_ref[...]
    o_ref[...] = jnp.dot(x, w_ref[...].astype(x.dtype), preferred_element_type=F32)


def mm(x, w, idx, n, bm, bn):
    m, k = x.shape
    bm = min(m, bm)
    bn = _tile(n, bn)
    return pl.pallas_call(
        _mm_kernel,
        grid=(m // bm, n // bn),
        in_specs=[
            pl.BlockSpec((bm, k), lambda i, j: (i, 0)),
            pl.BlockSpec((None, k, bn), lambda i, j: (idx, 0, j)),
        ],
        out_specs=pl.BlockSpec((bm, bn), lambda i, j: (i, j)),
        out_shape=jax.ShapeDtypeStruct((m, n), F32),
        compiler_params=_params("parallel", "arbitrary"),
        name="mm",
    )(x, w)


def _head_norm_kernel(x_ref, w_ref, avg_ref, o_ref):
    x = x_ref[...]
    ms = _dot_hi(x * x, avg_ref[...])
    o_ref[...] = x * lax.rsqrt(ms + EPS) * w_ref[...]


def head_norm(proj, w_cols, ncols):
    m = proj.shape[0]
    bm = min(m, 512)
    lane_head = jnp.arange(LANES) // SWA_HEAD_DIM
    avg = (lane_head[:, None] == lane_head[None, :]).astype(F32) / SWA_HEAD_DIM
    return pl.pallas_call(
        _head_norm_kernel,
        grid=(m // bm, ncols // LANES),
        in_specs=[
            pl.BlockSpec((bm, LANES), lambda i, j: (i, j)),
            pl.BlockSpec((1, LANES), lambda i, j: (0, j)),
            pl.BlockSpec((LANES, LANES), lambda i, j: (0, 0)),
        ],
        out_specs=pl.BlockSpec((bm, LANES), lambda i, j: (i, j)),
        out_shape=jax.ShapeDtypeStruct((m, ncols), F32),
        compiler_params=_params("parallel", "parallel"),
        name="head_norm",
    )(proj, w_cols, avg)


def _swa_kernel(sink_ref, q_ref, kc_ref, kp_ref, vc_ref, vp_ref, o_ref, *, n_heads, n_kv, always_prev):
    group = n_heads // n_kv
    w = WINDOW
    qi = lax.broadcasted_iota(jnp.int32, (w, w), 0)
    kj = lax.broadcasted_iota(jnp.int32, (w, w), 1)
    has_prev = jnp.logical_or(pl.program_id(1) > 0, always_prev)
    valid_p = jnp.logical_and(kj > qi, has_prev)
    valid_c = kj <= qi
    dist_p = (w + qi - kj).astype(F32)
    dist_c = (qi - kj).astype(F32)
    sm_scale = SWA_HEAD_DIM ** -0.5
    neg_inf = jnp.float32(-jnp.inf)
    for kvh in range(n_kv):
        ks = slice(kvh * SWA_HEAD_DIM, (kvh + 1) * SWA_HEAD_DIM)
        kc = kc_ref[0, :, ks]
        kp = kp_ref[0, :, ks]
        vc = vc_ref[0, :, ks]
        vp = vp_ref[0, :, ks]
        for g in range(group):
            h = kvh * group + g
            hs = slice(h * SWA_HEAD_DIM, (h + 1) * SWA_HEAD_DIM)
            slope = 2.0 ** (-8.0 * (h + 1) / n_heads)
            sink = sink_ref[h]
            q = q_ref[0, :, hs]
            s_p = _dot_nt(q, kp) * sm_scale - slope * dist_p
            s_c = _dot_nt(q, kc) * sm_scale - slope * dist_c
            s_p = jnp.where(valid_p, s_p, neg_inf)
            s_c = jnp.where(valid_c, s_c, neg_inf)
            m = jnp.maximum(jnp.maximum(jnp.max(s_p, axis=-1, keepdims=True),
                                        jnp.max(s_c, axis=-1, keepdims=True)), sink)
            p_p = jnp.exp(s_p - m)
            p_c = jnp.exp(s_c - m)
            denom = (jnp.sum(p_p, axis=-1, keepdims=True) + jnp.sum(p_c, axis=-1, keepdims=True)
                     + jnp.exp(sink - m))
            o = (_dot(p_p, vp) + _dot(p_c, vc)) / denom
            o_ref[0, :, hs] = o.astype(o_ref.dtype)


def swa_attention(sinks, q_arr, kc_arr, kp_arr, vc_arr, vp_arr, *, n_heads, n_kv, q_col, kc_col, kp_col,
                  vc_col, vp_col, prev_is_cache):
    b, l, _ = q_arr.shape
    nb = l // WINDOW
    qw = n_heads * SWA_HEAD_DIM
    kw = n_kv * SWA_HEAD_DIM
    if prev_is_cache:
        prev_map = lambda col: (lambda i, n: (i, 0, col))
    else:
        prev_map = lambda col: (lambda i, n: (i, jnp.maximum(n - 1, 0), col))
    return pl.pallas_call(
        functools.partial(_swa_kernel, n_heads=n_heads, n_kv=n_kv, always_prev=prev_is_cache),
        grid=(b, nb),
        in_specs=[
            pl.BlockSpec(memory_space=pltpu.SMEM),
            pl.BlockSpec((1, WINDOW, qw), lambda i, n: (i, n, q_col)),
            pl.BlockSpec((1, WINDOW, kw), lambda i, n: (i, n, kc_col)),
            pl.BlockSpec((1, WINDOW, kw), prev_map(kp_col)),
            pl.BlockSpec((1, WINDOW, kw), lambda i, n: (i, n, vc_col)),
            pl.BlockSpec((1, WINDOW, kw), prev_map(vp_col)),
        ],
        out_specs=pl.BlockSpec((1, WINDOW, qw), lambda i, n: (i, n, 0)),
        out_shape=jax.ShapeDtypeStruct((b, l, qw), MM_DTYPE),
        compiler_params=_params("parallel", "arbitrary"),
        name="swa_attention",
    )(sinks, q_arr, kc_arr, kp_arr, vc_arr, vp_arr)


def _conv_silu_kernel(xp_ref, w_ref, b_ref, o_ref):
    l = o_ref.shape[1]
    y = sum(xp_ref[0, i:i + l, :] * w_ref[i:i + 1, :] for i in range(CONV_W)) + b_ref[...]
    o_ref[0] = _silu(y)


def conv_silu(xp, w, b):
    bsz, lp, c = xp.shape
    l = lp - (CONV_W - 1)
    bc = _tile(c, 512)
    return pl.pallas_call(
        _conv_silu_kernel,
        grid=(bsz, c // bc),
        in_specs=[
            pl.BlockSpec((1, lp, bc), lambda i, j: (i, 0, j)),
            pl.BlockSpec((CONV_W, bc), lambda i, j: (0, j)),
            pl.BlockSpec((1, bc), lambda i, j: (0, j)),
        ],
        out_specs=pl.BlockSpec((1, l, bc), lambda i, j: (i, 0, j)),
        out_shape=jax.ShapeDtypeStruct((bsz, l, c), F32),
        compiler_params=_params("parallel", "parallel"),
        name="conv_silu",
    )(xp, w, b.reshape(1, c))


def _conv_silu_long_kernel(x_ref, buf_ref, w_ref, b_ref, o_ref, head_scr):
    x = x_ref[0]
    acc = x * w_ref[CONV_W - 1:CONV_W, :] + b_ref[...]
    for s in range(1, CONV_W):
        acc = acc + pltpu.roll(x, s, 0) * w_ref[CONV_W - 1 - s:CONV_W - s, :]
    o_ref[0] = _silu(acc)
    lo = SUBLANES - (CONV_W - 1)
    head_scr[lo:SUBLANES, :] = buf_ref[0]
    head_scr[SUBLANES:2 * SUBLANES, :] = x[0:SUBLANES]
    yh = sum(head_scr[lo + i:lo + i + SUBLANES, :] * w_ref[i:i + 1, :] for i in range(CONV_W)) + b_ref[...]
    o_ref[0, 0:SUBLANES, :] = _silu(yh)


def conv_silu_long(x_arr, col0, c, buf, w, b):
    bsz, l, _ = x_arr.shape
    bc = _tile(c, 512)
    assert col0 % bc == 0 and l % SUBLANES == 0
    j0 = col0 // bc
    return pl.pallas_call(
        _conv_silu_long_kernel,
        grid=(bsz, c // bc),
        in_specs=[
            pl.BlockSpec((1, l, bc), lambda i, j: (i, 0, j0 + j)),
            pl.BlockSpec((1, CONV_W - 1, bc), lambda i, j: (i, 0, j)),
            pl.BlockSpec((CONV_W, bc), lambda i, j: (0, j)),
            pl.BlockSpec((1, bc), lambda i, j: (0, j)),
        ],
        out_specs=pl.BlockSpec((1, l, bc), lambda i, j: (i, 0, j)),
        out_shape=jax.ShapeDtypeStruct((bsz, l, c), F32),
        scratch_shapes=[pltpu.VMEM((2 * SUBLANES, bc), F32)],
        compiler_params=_params("parallel", "parallel"),
        name="conv_silu_long",
    )(x_arr, buf, w, b.reshape(1, c))


def _unit_lower_inverses(a_list, r, c):
    ri = lax.broadcasted_iota(jnp.int32, (r, r), 0)
    ci = lax.broadcasted_iota(jnp.int32, (r, r), 1)
    eye = (ri == ci).astype(F32)

    def same_block(bits):
        return lax.shift_right_logical(ri, bits) == lax.shift_right_logical(ci, bits)

    n1 = [jnp.where(same_block(3), a, 0.0) for a in a_list]
    n2 = [_dot3(n, n) for n in n1]
    n4 = [_dot3(n, n) for n in n2]
    t = [_dot3(eye - a, eye + b) for a, b in zip(n1, n2)]
    t = [_dot3(a, eye + b) for a, b in zip(t, n4)]
    bits = 3
    while (1 << bits) < c:
        pair = jnp.logical_and(same_block(bits + 1), jnp.logical_not(same_block(bits)))
        left = [_dot3(ti, jnp.where(pair, a, 0.0)) for ti, a in zip(t, a_list)]
        t = [ti - _dot3(li, ti) for ti, li in zip(t, left)]
        bits += 1
    return t


def _gdn_prep_kernel(alog_ref, dtb_ref, q_ref, k_ref, v_ref, ab_ref, w_ref, u0_ref, qe_ref, kw_ref, qk_ref, e_ref,
                     *, c, hb, n_heads, n_valid, lp):
    r = q_ref.shape[1]
    t = pl.program_id(1)
    h0 = pl.program_id(2) * hb
    bits = c.bit_length() - 1
    ri = lax.broadcasted_iota(jnp.int32, (r, r), 0)
    ci = lax.broadcasted_iota(jnp.int32, (r, r), 1)
    same = lax.shift_right_logical(ri, bits) == lax.shift_right_logical(ci, bits)
    eye = ri == ci
    lower = jnp.logical_and(ri >= ci, same)
    strict = jnp.logical_and(ri > ci, same)
    upper = jnp.logical_and(ri <= ci, same)
    lane = lax.broadcasted_iota(jnp.int32, (r, LANES), 1)
    row = lax.broadcasted_iota(jnp.int32, (r, 1), 0)
    ab = ab_ref[0]
    heads = range(hb)
    sls = [slice(hh * GDN_DK, (hh + 1) * GDN_DK) for hh in heads]
    ks, betas, a_mats = [], [], []
    for hh in heads:
        h = h0 + hh
        q = q_ref[0, :, sls[hh]]
        k = k_ref[0, :, sls[hh]]
        q = q * lax.rsqrt(jnp.sum(q * q, axis=-1, keepdims=True) + EPS) * (GDN_DK ** -0.5)
        k = k * lax.rsqrt(jnp.sum(k * k, axis=-1, keepdims=True) + EPS)
        a_col = jnp.sum(jnp.where(lane == h, ab, 0.0), axis=-1, keepdims=True)
        b_col = jnp.sum(jnp.where(lane == h + n_heads, ab, 0.0), axis=-1, keepdims=True)
        g_col = -jnp.exp(alog_ref[h]) * _softplus(a_col + dtb_ref[h])
        beta = jax.nn.sigmoid(b_col)
        if n_valid < lp:
            live = (row + t * r) < n_valid
            g_col = jnp.where(live, g_col, 0.0)
            beta = jnp.where(live, beta, 0.0)
        g_bc = jnp.broadcast_to(g_col, (r, r))
        g_row = jnp.sum(jnp.where(eye, g_bc, 0.0), axis=0, keepdims=True)
        gc_row = jnp.sum(jnp.where(upper, g_bc, 0.0), axis=0, keepdims=True)
        gc_col = jnp.sum(jnp.where(lower, jnp.broadcast_to(g_row, (r, r)), 0.0), axis=1, keepdims=True)
        decay = jnp.where(lower, jnp.exp(gc_col - gc_row), 0.0)
        eg = jnp.exp(gc_col)
        qe_ref[0, :, sls[hh]] = (q * eg).astype(qe_ref.dtype)
        qk = _dot_nt(q, k) * decay
        qkc = qk[:, 0:c]
        g_end = jnp.zeros((r, 1), F32)
        for j in range(r // c):
            if j:
                qkc = qkc + qk[:, j * c:(j + 1) * c]
            ge = gc_col[j * c + c - 1:(j + 1) * c, :]
            g_end = jnp.where(jnp.logical_and(row >= j * c, row < (j + 1) * c), ge, g_end)
            e_ref[0, j * SUBLANES:(j + 1) * SUBLANES, sls[hh]] = jnp.broadcast_to(jnp.exp(ge),
                                                                                 (SUBLANES, GDN_DV))
        qk_ref[0, :, hh * c:(hh + 1) * c] = qkc.astype(qk_ref.dtype)
        kw_ref[0, :, sls[hh]] = (k * jnp.exp(g_end - gc_col)).astype(kw_ref.dtype)
        a_mats.append(jnp.where(strict, _dot_nt(k, k) * beta * decay, 0.0))
        ks.append(k * (beta * eg))
        betas.append(beta)
    t_invs = _unit_lower_inverses(a_mats, r, c)
    ws = [_dot3(ti, kb) for ti, kb in zip(t_invs, ks)]
    u0s = [_dot3(ti, v_ref[0, :, sls[hh]] * betas[hh]) for hh, ti in zip(heads, t_invs)]
    for hh in heads:
        w_ref[0, :, sls[hh]] = ws[hh].astype(w_ref.dtype)
        u0_ref[0, :, sls[hh]] = u0s[hh]


def _gdn_scan_kernel(w_ref, u0_ref, qe_ref, kw_ref, qk_ref, e_ref, gate_ref, nw_ref, s0_ref, o_ref, s_ref,
                     *, c, hb):
    lp = w_ref.shape[1]
    for hh in range(hb):
        s_ref[0, hh] = s0_ref[0, hh]

    def chunk(t, carry):
        rows = pl.ds(pl.multiple_of(t * c, c), c)
        erow = pl.ds(pl.multiple_of(t * SUBLANES, SUBLANES), 1)
        heads = range(hb)
        sls = [slice(hh * GDN_DV, (hh + 1) * GDN_DV) for hh in heads]
        ss = [s_ref[0, hh] for hh in heads]
        us = [u0_ref[0, rows, sls[hh]] - _dot(w_ref[0, rows, sls[hh]], ss[hh]) for hh in heads]
        for hh in heads:
            s_ref[0, hh] = e_ref[0, erow, sls[hh]] * ss[hh] + _dot_tn(kw_ref[0, rows, sls[hh]], us[hh])
        os_ = [_dot(qe_ref[0, rows, sls[hh]], ss[hh]) + _dot(qk_ref[0, rows, hh * c:(hh + 1) * c], us[hh])
               for hh in heads]
        for hh in heads:
            o = os_[hh]
            o = o * lax.rsqrt(jnp.mean(o * o, axis=-1, keepdims=True) + EPS) * nw_ref[...]
            o_ref[0, rows, sls[hh]] = (o * _silu(gate_ref[0, rows, sls[hh]])).astype(o_ref.dtype)
        return carry

    lax.fori_loop(0, lp // c, chunk, 0)


def gated_deltanet(a_log, dt_bias, qkv, tail, norm_w, s0, *, n_valid):
    bsz, lp, _ = qkv.shape
    n_heads = s0.shape[1]
    c = min(GDN_CHUNK, lp)
    r = min(2 * GDN_CHUNK, lp)
    full = c == GDN_CHUNK
    hb1 = min(n_heads, 8) if full else n_heads
    hb2 = min(n_heads, 4) if full else n_heads
    nhb = n_heads // hb1
    hd = n_heads * GDN_DK
    ab_col = hd // LANES
    blk = lambda off: pl.BlockSpec((1, r, hb1 * GDN_DK), lambda i, t, h: (i, t, off + h))
    e_rows = (r // c) * SUBLANES
    w, u0, qe, kw, qk, e = pl.pallas_call(
        functools.partial(_gdn_prep_kernel, c=c, hb=hb1, n_heads=n_heads, n_valid=n_valid, lp=lp),
        grid=(bsz, lp // r, nhb),
        in_specs=[
            pl.BlockSpec(memory_space=pltpu.SMEM),
            pl.BlockSpec(memory_space=pltpu.SMEM),
            blk(0), blk(nhb), blk(2 * nhb),
            pl.BlockSpec((1, r, LANES), lambda i, t, h: (i, t, ab_col)),
        ],
        out_specs=[
            blk(0), blk(0), blk(0), blk(0),
            pl.BlockSpec((1, r, hb1 * c), lambda i, t, h: (i, t, h)),
            pl.BlockSpec((1, e_rows, hb1 * GDN_DV), lambda i, t, h: (i, t, h)),
        ],
        out_shape=[
            jax.ShapeDtypeStruct((bsz, lp, hd), MM_DTYPE),
            jax.ShapeDtypeStruct((bsz, lp, hd), F32),
            jax.ShapeDtypeStruct((bsz, lp, hd), MM_DTYPE),
            jax.ShapeDtypeStruct((bsz, lp, hd), MM_DTYPE),
            jax.ShapeDtypeStruct((bsz, lp, n_heads * c), MM_DTYPE),
            jax.ShapeDtypeStruct((bsz, (lp // c) * SUBLANES, hd), F32),
        ],
        compiler_params=_params("parallel", "parallel", "parallel"),
        name="gdn_prep",
    )(a_log, dt_bias, qkv, qkv, qkv, tail)
    blk2 = lambda width: pl.BlockSpec((1, lp, hb2 * width), lambda i, h: (i, 0, h))
    return pl.pallas_call(
        functools.partial(_gdn_scan_kernel, c=c, hb=hb2),
        grid=(bsz, n_heads // hb2),
        in_specs=[
            blk2(GDN_DK), blk2(GDN_DV), blk2(GDN_DK), blk2(GDN_DK), blk2(c),
            pl.BlockSpec((1, (lp // c) * SUBLANES, hb2 * GDN_DV), lambda i, h: (i, 0, h)),
            blk2(GDN_DV),
            pl.BlockSpec((1, GDN_DV), lambda i, h: (0, 0)),
            pl.BlockSpec((1, hb2, GDN_DK, GDN_DV), lambda i, h: (i, h, 0, 0)),
        ],
        out_specs=[
            blk2(GDN_DV),
            pl.BlockSpec((1, hb2, GDN_DK, GDN_DV), lambda i, h: (i, h, 0, 0)),
        ],
        out_shape=[
            jax.ShapeDtypeStruct((bsz, lp, n_heads * GDN_DV), MM_DTYPE),
            jax.ShapeDtypeStruct(s0.shape, F32),
        ],
        compiler_params=_params("parallel", "parallel"),
        name="gdn_scan",
    )(w, u0, qe, kw, qk, e, tail, norm_w.reshape(1, GDN_DV), s0)


def _ssd_kernel(x_ref, b_ref, c_ref, z_ref, dt_ref, alog_ref, dtb_ref, d_ref, nw_ref, s0_ref, y_ref, s_ref,
                y_scr, *, hpg, lp, n_valid):
    g = pl.program_id(1)
    t = pl.program_id(2)
    c = x_ref.shape[1]
    p = SSD_HEAD_DIM

    @pl.when(t == 0)
    def _():
        s_ref[0, 0] = s0_ref[0, 0]

    ri = lax.broadcasted_iota(jnp.int32, (c, c), 0)
    ci = lax.broadcasted_iota(jnp.int32, (c, c), 1)
    lower = ri >= ci
    tri = lower.astype(F32)
    hl = lax.broadcasted_iota(jnp.int32, (LANES, LANES), 0)
    zl = lax.broadcasted_iota(jnp.int32, (LANES, LANES), 1)
    sel = jnp.logical_and(hl == g * hpg + zl, zl < hpg).astype(F32)

    xs = x_ref[0]
    bm = b_ref[0]
    cm = c_ref[0]
    dt = _softplus(dt_ref[0] + dtb_ref[...])
    if n_valid < lp:
        live = (lax.broadcasted_iota(jnp.int32, (c, 1), 0) + t * c) < n_valid
        dt = jnp.where(live, dt, 0.0)
    gcum = _dot_hi(tri, dt * (-jnp.exp(alog_ref[...])))
    gcg = _dot_hi(gcum, sel)
    gcg_t = gcg.T
    eh = lax.broadcasted_iota(jnp.int32, (LANES, hpg * p), 0)
    ec = lax.shift_right_logical(lax.broadcasted_iota(jnp.int32, (LANES, hpg * p), 1), p.bit_length() - 1)
    spread = (eh == g * hpg + ec).astype(F32)
    dt_x = _dot3(dt, spread)
    eg_x = _dot3(jnp.exp(gcum), spread)
    ew_x = _dot3(jnp.exp(gcum[c - 1:c, :] - gcum), spread)
    d_x = _dot3(jnp.broadcast_to(d_ref[...], (SUBLANES, LANES)), spread)[0:1, :]
    cb = _dot_nt(cm, bm)
    s = s_ref[0, 0]
    xdt = xs * dt_x
    y_scr[...] = _dot(cm, s) * eg_x + d_x * xs
    for z in range(hpg):
        sl = slice(z * p, (z + 1) * p)
        decay = jnp.where(lower, jnp.exp(gcg[:, z:z + 1] - gcg_t[z:z + 1, :]), 0.0)
        y_scr[:, sl] = y_scr[:, sl] + _dot(cb * decay, xdt[:, sl])
    s_ref[0, 0] = eg_x[c - 1:c, :] * s + _dot_tn(bm, xdt * ew_x)
    y = y_scr[...] * _silu(z_ref[0])
    y = y * lax.rsqrt(jnp.mean(y * y, axis=-1, keepdims=True) + EPS) * nw_ref[...]
    y_ref[0] = y.astype(y_ref.dtype)


def ssd_mixer(xbc, z_arr, dt_arr, a_log, dt_bias, d_skip, norm_w, s0_t, *, n_groups, z_col, dt_col, n_valid):
    bsz, lp, _ = xbc.shape
    n_heads = a_log.shape[0]
    assert n_heads == LANES
    hpg = n_heads // n_groups
    gw = hpg * SSD_HEAD_DIM
    d_inner = n_heads * SSD_HEAD_DIM
    c = min(SSD_CHUNK, lp)
    nc = lp // c
    assert gw % LANES == 0 and d_inner % gw == 0
    b_col0 = d_inner // SSD_STATE
    vec = lambda a: a.reshape(1, n_heads)
    return pl.pallas_call(
        functools.partial(_ssd_kernel, hpg=hpg, lp=lp, n_valid=n_valid),
        grid=(bsz, n_groups, nc),
        in_specs=[
            pl.BlockSpec((1, c, gw), lambda i, g, t: (i, t, g)),
            pl.BlockSpec((1, c, SSD_STATE), lambda i, g, t: (i, t, b_col0 + g)),
            pl.BlockSpec((1, c, SSD_STATE), lambda i, g, t: (i, t, b_col0 + n_groups + g)),
            pl.BlockSpec((1, c, gw), lambda i, g, t: (i, t, z_col + g)),
            pl.BlockSpec((1, c, LANES), lambda i, g, t: (i, t, dt_col)),
            pl.BlockSpec((1, LANES), lambda i, g, t: (0, 0)),
            pl.BlockSpec((1, LANES), lambda i, g, t: (0, 0)),
            pl.BlockSpec((1, LANES), lambda i, g, t: (0, 0)),
            pl.BlockSpec((1, gw), lambda i, g, t: (0, g)),
            pl.BlockSpec((1, 1, SSD_STATE, gw), lambda i, g, t: (i, g, 0, 0)),
        ],
        out_specs=[
            pl.BlockSpec((1, c, gw), lambda i, g, t: (i, t, g)),
            pl.BlockSpec((1, 1, SSD_STATE, gw), lambda i, g, t: (i, g, 0, 0)),
        ],
        out_shape=[
            jax.ShapeDtypeStruct((bsz, lp, d_inner), MM_DTYPE),
            jax.ShapeDtypeStruct(s0_t.shape, F32),
        ],
        scratch_shapes=[pltpu.VMEM((c, gw), F32)],
        compiler_params=_params("parallel", "parallel", "arbitrary"),
        name="ssd_mixer",
    )(xbc, xbc, xbc, z_arr, dt_arr, vec(a_log), vec(dt_bias), vec(d_skip), norm_w.reshape(1, d_inner), s0_t)


def _pad_rows(a, lp):
    return jnp.pad(a, ((0, 0), (0, lp - a.shape[1]), (0, 0)))


def _round_up(n, mult):
    return -(-n // mult) * mult


def _even_tail_weights(ev_w_in, o_ab, n_heads_b):
    o_g = o_ab + 2 * n_heads_b
    ab = ev_w_in[:, :, o_ab:o_g]
    gate = ev_w_in[:, :, o_g:]
    width = _round_up(gate.shape[2] + LANES, 3 * LANES)
    pad = jnp.zeros(ab.shape[:2] + (width - gate.shape[2] - 2 * n_heads_b,), ev_w_in.dtype)
    return jnp.concatenate([gate, ab, pad], axis=-1)


def _trunk(x3, mod_all, p, cache):
    bsz, l, d = x3.shape
    m = bsz * l
    x = x3.reshape(m, d)
    depth = p['norm_w'].shape[0]
    n_heads_a = p['swa_sinks'].shape[1]
    n_kv = p['n_kv']
    n_heads_b = p['gdn_a_log'].shape[1]
    n_heads_c = p['ssm_a_log'].shape[1]
    d_inner = n_heads_c * SSD_HEAD_DIM
    n_groups = (p['ssm_conv_w'].shape[2] - d_inner) // (2 * SSD_STATE)
    hpg = n_heads_c // n_groups
    qa = n_heads_a * SWA_HEAD_DIM
    kv = n_kv * SWA_HEAD_DIM
    conv_b = n_heads_b * (2 * GDN_DK + GDN_DV)
    conv_c = d_inner + 2 * n_groups * SSD_STATE
    expand = cache is not None
    out = {'k': [], 'v': [], 'gdn': [], 'gdn_conv': [], 'ssm': [], 'ssm_conv': []}

    def mod_of(layer, idx):
        v = mod_all[layer, :, idx]
        if expand:
            return jnp.repeat(v, l, axis=0).reshape(1, m, d)
        return v.reshape(bsz, 1, d)

    o_qkv = qa + 2 * kv
    o_ab = o_qkv + conv_b
    long_seq = l % SSD_CHUNK == 0 and l % GDN_CHUNK == 0
    lp = l if long_seq else _round_up(l, SUBLANES)
    for layer in range(depth):
        mo = functools.partial(mod_of, layer)
        h = ada_norm(x, p['norm_w'][layer, 0], mo(0), mo(1), l)
        act = swiglu_up(h, p['ffn_w_in'], 2 * layer)
        x = mm_residual([act], p['ffn_w_out'], 2 * layer, x, mo(2), 0.5, l, 512, 512)
        h = ada_norm(x, p['norm_w'][layer, 1], mo(3), mo(4), l)
        if layer % 2 == 0:
            e = layer // 2
            proj = mm(h, p['ev_w_in'], e, o_ab, 1024, 512)
            proj3 = proj.reshape(bsz, l, o_ab)
            tail3 = mm(h, p['ev_w_tail'], e, p['ev_w_tail'].shape[2], 1024, 384).reshape(bsz, l, -1)
            w_cols = jnp.concatenate([jnp.tile(p['swa_q_norm'][e], n_heads_a),
                                      jnp.tile(p['swa_k_norm'][e], n_kv)]).reshape(1, qa + kv)
            qk = head_norm(proj, w_cols, qa + kv).reshape(bsz, l, qa + kv)
            k_new = qk[:, :, qa:]
            v_new = proj3[:, :, qa + kv:qa + 2 * kv]
            if cache is None:
                o_a = swa_attention(p['swa_sinks'][e], qk, qk, qk, proj3, proj3, n_heads=n_heads_a, n_kv=n_kv,
                                    q_col=0, kc_col=qa // kv, kp_col=qa // kv, vc_col=(qa + kv) // kv,
                                    vp_col=(qa + kv) // kv, prev_is_cache=False)
                wb = min(WINDOW, l)
                out['k'].append(k_new[:, l - wb:].reshape(bsz, wb, n_kv, SWA_HEAD_DIM))
                out['v'].append(v_new[:, l - wb:].reshape(bsz, wb, n_kv, SWA_HEAD_DIM))
                conv0 = jnp.zeros((bsz, CONV_W - 1, conv_b), F32)
                s0 = jnp.zeros((bsz, n_heads_b, GDN_DK, GDN_DV), F32)
            else:
                kbuf = cache['k'][e].reshape(bsz, -1, kv)
                vbuf = cache['v'][e].reshape(bsz, -1, kv)
                o_a = swa_attention(p['swa_sinks'][e], _pad_rows(qk, WINDOW), _pad_rows(qk, WINDOW), kbuf,
                                    _pad_rows(v_new, WINDOW), vbuf, n_heads=n_heads_a, n_kv=n_kv,
                                    q_col=0, kc_col=qa // kv, kp_col=0, vc_col=0, vp_col=0,
                                    prev_is_cache=True)[:, :l]
                out['k'].append(jnp.concatenate([kbuf, k_new], axis=1)[:, l:].reshape(bsz, -1, n_kv, SWA_HEAD_DIM))
                out['v'].append(jnp.concatenate([vbuf, v_new], axis=1)[:, l:].reshape(bsz, -1, n_kv, SWA_HEAD_DIM))
                conv0 = cache['gdn_conv'][e]
                s0 = cache['gdn'][e]
            if long_seq:
                out['gdn_conv'].append(proj3[:, l - (CONV_W - 1):, o_qkv:o_ab])
                qkv = conv_silu_long(proj3, o_qkv, conv_b, conv0, p['gdn_conv_w'][e], p['gdn_conv_b'][e])
            else:
                xp = jnp.concatenate([conv0, proj3[:, :, o_qkv:o_ab]], axis=1)
                out['gdn_conv'].append(xp[:, xp.shape[1] - (CONV_W - 1):])
                qkv = _pad_rows(conv_silu(xp, p['gdn_conv_w'][e], p['gdn_conv_b'][e]), lp)
                tail3 = _pad_rows(tail3, lp)
            o_b, s_new = gated_deltanet(p['gdn_a_log'][e], p['gdn_dt_bias'][e], qkv, tail3, p['gdn_norm_w'][e], s0,
                                        n_valid=l)
            out['gdn'].append(s_new)
            x = mm_residual([o_a.reshape(m, -1), o_b[:, :l].reshape(m, -1)], p['ev_w_out'], e, x, mo(5), 1.0, l,
                            1024, 512)
        else:
            oi = layer // 2
            proj = mm(h, p['od_w_in'], oi, p['od_w_in'].shape[2], 1024, 640)
            proj3 = proj.reshape(bsz, l, -1)
            if cache is None:
                conv0 = jnp.zeros((bsz, CONV_W - 1, conv_c), F32)
                s0_t = jnp.zeros((bsz, n_groups, SSD_STATE, hpg * SSD_HEAD_DIM), F32)
            else:
                conv0 = cache['ssm_conv'][oi]
                s0_t = (cache['ssm'][oi].reshape(bsz, n_groups, hpg * SSD_HEAD_DIM, SSD_STATE)
                        .transpose(0, 1, 3, 2))
            o_dt = d_inner + conv_c
            if long_seq:
                out['ssm_conv'].append(proj3[:, l - (CONV_W - 1):, d_inner:o_dt])
                xbc = conv_silu_long(proj3, d_inner, conv_c, conv0, p['ssm_conv_w'][oi], p['ssm_conv_b'][oi])
                zdt = proj3
            else:
                xp = jnp.concatenate([conv0, proj3[:, :, d_inner:o_dt]], axis=1)
                out['ssm_conv'].append(xp[:, xp.shape[1] - (CONV_W - 1):])
                xbc = _pad_rows(conv_silu(xp, p['ssm_conv_w'][oi], p['ssm_conv_b'][oi]), lp)
                zdt = _pad_rows(proj3, lp)
            y, s_t = ssd_mixer(xbc, zdt, zdt, p['ssm_a_log'][oi], p['ssm_dt_bias'][oi], p['ssm_d'][oi],
                               p['ssm_norm_w'][oi], s0_t, n_groups=n_groups, z_col=0, dt_col=o_dt // LANES,
                               n_valid=l)
            out['ssm'].append(s_t.transpose(0, 1, 3, 2).reshape(bsz, n_heads_c, SSD_HEAD_DIM, SSD_STATE))
            x = mm_residual([y[:, :l].reshape(m, d_inner)], p['od_w_out'], oi, x, mo(5), 1.0, l, 512, 512)
        h = ada_norm(x, p['norm_w'][layer, 2], mo(6), mo(7), l)
        act = swiglu_up(h, p['ffn_w_in'], 2 * layer + 1)
        x = mm_residual([act], p['ffn_w_out'], 2 * layer + 1, x, mo(8), 0.5, l, 512, 512)
    return (x.reshape(bsz, l, d), jnp.stack(out['k']), jnp.stack(out['v']), jnp.stack(out['gdn']),
            jnp.stack(out['gdn_conv']), jnp.stack(out['ssm']), jnp.stack(out['ssm_conv']))


def kernel(x_prompt, x_sample, c_prompt, c_sample, cache_swa_k, cache_swa_v, state_gdn, state_gdn_conv, state_ssm, state_ssm_conv, ada_w, ada_b, norm_w, ffn_w_in, ffn_w_out, ev_w_in, ev_w_out, swa_q_norm, swa_k_norm, swa_sinks, gdn_conv_w, gdn_conv_b, gdn_a_log, gdn_dt_bias, gdn_norm_w, od_w_in, od_w_out, ssm_conv_w, ssm_conv_b, ssm_a_log, ssm_dt_bias, ssm_d, ssm_norm_w):
    depth, d = norm_w.shape[0], norm_w.shape[2]
    n_kv = cache_swa_k.shape[3]
    n_heads_a = swa_sinks.shape[1]
    n_heads_b = gdn_a_log.shape[1]
    bp, bs = c_prompt.shape[0], c_sample.shape[0]
    r = bp + bs
    r_pad = -(-r // 16) * 16
    c_all = jnp.pad(jnp.concatenate([c_prompt, c_sample], axis=0), ((0, r_pad - r), (0, 0)))
    mod = ada_modulation(c_all, ada_w, ada_b).reshape(depth, r_pad, N_MOD, d)
    o_ab = (n_heads_a + 2 * n_kv) * SWA_HEAD_DIM + n_heads_b * (2 * GDN_DK + GDN_DV)
    f2 = ffn_w_in.shape[3]
    p = {
        'norm_w': norm_w,
        'ffn_w_in': ffn_w_in.reshape(2 * depth, d, f2),
        'ffn_w_out': ffn_w_out.astype(MM_DTYPE).reshape(2 * depth, f2 // 2, d),
        'ev_w_in': ev_w_in, 'ev_w_tail': _even_tail_weights(ev_w_in, o_ab, n_heads_b),
        'ev_w_out': ev_w_out.astype(MM_DTYPE),
        'swa_q_norm': swa_q_norm, 'swa_k_norm': swa_k_norm, 'swa_sinks': swa_sinks, 'n_kv': n_kv,
        'gdn_conv_w': gdn_conv_w, 'gdn_conv_b': gdn_conv_b, 'gdn_a_log': gdn_a_log,
        'gdn_dt_bias': gdn_dt_bias, 'gdn_norm_w': gdn_norm_w,
        'od_w_in': od_w_in, 'od_w_out': od_w_out.astype(MM_DTYPE),
        'ssm_conv_w': ssm_conv_w, 'ssm_conv_b': ssm_conv_b, 'ssm_a_log': ssm_a_log,
        'ssm_dt_bias': ssm_dt_bias, 'ssm_d': ssm_d, 'ssm_norm_w': ssm_norm_w,
    }
    cache = {'k': cache_swa_k, 'v': cache_swa_v, 'gdn': state_gdn, 'gdn_conv': state_gdn_conv,
             'ssm': state_ssm, 'ssm_conv': state_ssm_conv}
    y_p, k_p, v_p, gdn_p, gdn_conv_p, ssm_p, ssm_conv_p = _trunk(x_prompt, mod[:, :bp], p, None)
    y_s, k_s, v_s, gdn_s, gdn_conv_s, ssm_s, ssm_conv_s = _trunk(x_sample, mod[:, bp:r], p, cache)
    return (y_p, y_s, k_p, v_p, gdn_p, gdn_conv_p, ssm_p, ssm_conv_p,
            k_s, v_s, gdn_s, gdn_conv_s, ssm_s, ssm_conv_s)
```

```python
import functools
import math

import jax
import jax.numpy as jnp
from jax import lax
from jax.experimental import pallas as pl
from jax.experimental.pallas import tpu as pltpu

EPS = 1e-6
N_MOD = 9
SWA_HEAD_DIM = 64
WINDOW = 128
GDN_DK = 128
GDN_DV = 128
GDN_CHUNK = 64
CONV_W = 4
SSD_HEAD_DIM = 64
SSD_STATE = 128
SSD_CHUNK = 128

LANES = 128
SUBLANES = 8
VMEM_LIMIT_BYTES = 56 * 1024 * 1024

MM_DTYPE = jnp.bfloat16
HI = lax.Precision.HIGHEST
F32 = jnp.float32


def _dot(a, b):
    return jnp.dot(a.astype(MM_DTYPE), b.astype(MM_DTYPE), preferred_element_type=F32)


def _dot_nt(a, b):
    return lax.dot_general(a.astype(MM_DTYPE), b.astype(MM_DTYPE), (((1,), (1,)), ((), ())),
                           preferred_element_type=F32)


def _dot_tn(a, b):
    return lax.dot_general(a.astype(MM_DTYPE), b.astype(MM_DTYPE), (((0,), (0,)), ((), ())),
                           preferred_element_type=F32)


def _dot_hi(a, b):
    return jnp.dot(a, b, preferred_element_type=F32, precision=HI)


def _hi_lo(a):
    hi = a.astype(MM_DTYPE)
    return hi, (a - hi.astype(F32)).astype(MM_DTYPE)


def _dot3(a, b):
    ah, al = _hi_lo(a)
    bh, bl = _hi_lo(b)
    return (jnp.dot(ah, bh, preferred_element_type=F32)
            + (jnp.dot(ah, bl, preferred_element_type=F32) + jnp.dot(al, bh, preferred_element_type=F32)))


def _silu(x):
    return x * jax.nn.sigmoid(x)


def _softplus(x):
    return jnp.maximum(x, 0.0) + jnp.log1p(jnp.exp(-jnp.abs(x)))


def _params(*sem):
    return pltpu.CompilerParams(dimension_semantics=sem, vmem_limit_bytes=VMEM_LIMIT_BYTES)


def _tile(n, target):
    if n <= target:
        return n
    t = (target // LANES) * LANES
    while t > LANES and n % t:
        t -= LANES
    assert n % t == 0, (n, target)
    return t


def _ada_kernel(c_ref, w_ref, b_ref, o_ref):
    a = _silu(c_ref[...])
    o_ref[...] = _dot(a, w_ref[...]) + b_ref[...]


def ada_modulation(c_all, ada_w, ada_b):
    depth, d, n = ada_w.shape
    r = c_all.shape[0]
    bn = _tile(n, 512)
    return pl.pallas_call(
        _ada_kernel,
        grid=(depth, n // bn),
        in_specs=[
            pl.BlockSpec((r, d), lambda l, j: (0, 0)),
            pl.BlockSpec((None, d, bn), lambda l, j: (l, 0, j)),
            pl.BlockSpec((None, 1, bn), lambda l, j: (l, 0, j)),
        ],
        out_specs=pl.BlockSpec((None, r, bn), lambda l, j: (l, 0, j)),
        out_shape=jax.ShapeDtypeStruct((depth, r, n), F32),
        compiler_params=_params("parallel", "parallel"),
        name="ada_modulation",
    )(c_all, ada_w, ada_b.reshape(depth, 1, n))


def _ada_norm_kernel(x_ref, w_ref, shift_ref, scale_ref, o_ref):
    x = x_ref[...]
    h = x * lax.rsqrt(jnp.mean(x * x, axis=-1, keepdims=True) + EPS) * w_ref[...]
    o_ref[...] = (h * (1.0 + scale_ref[0]) + shift_ref[0]).astype(o_ref.dtype)


def _mod_spec(mod, rows_per_seq, bm, bn, two_d_grid):
    g, r, _ = mod.shape
    if r == 1:
        assert rows_per_seq % bm == 0
        per = rows_per_seq // bm
        if two_d_grid:
            return pl.BlockSpec((1, 1, bn), lambda i, j: (i // per, 0, j))
        return pl.BlockSpec((1, 1, bn), lambda i: (i // per, 0, 0))
    assert g == 1 and r == bm
    if two_d_grid:
        return pl.BlockSpec((1, r, bn), lambda i, j: (0, 0, j))
    return pl.BlockSpec((1, r, bn), lambda i: (0, 0, 0))


def _row_block(m, target, mod, rows_per_seq):
    return min(m, target, rows_per_seq) if mod.shape[1] == 1 else min(m, target)


def ada_norm(x, w, shift, scale, rows_per_seq):
    m, d = x.shape
    bm = _row_block(m, 256, shift, rows_per_seq)
    return pl.pallas_call(
        _ada_norm_kernel,
        grid=(m // bm,),
        in_specs=[
            pl.BlockSpec((bm, d), lambda i: (i, 0)),
            pl.BlockSpec((1, d), lambda i: (0, 0)),
            _mod_spec(shift, rows_per_seq, bm, d, False),
            _mod_spec(scale, rows_per_seq, bm, d, False),
        ],
        out_specs=pl.BlockSpec((bm, d), lambda i: (i, 0)),
        out_shape=jax.ShapeDtypeStruct((m, d), MM_DTYPE),
        compiler_params=_params("parallel"),
        name="ada_norm",
    )(x, w.reshape(1, d), shift, scale)


def _swiglu_kernel(x_ref, wg_ref, wu_ref, o_ref):
    x = x_ref[...]
    g = jnp.dot(x, wg_ref[...].astype(x.dtype), preferred_element_type=F32)
    u = jnp.dot(x, wu_ref[...].astype(x.dtype), preferred_element_type=F32)
    o_ref[...] = (_silu(g) * u).astype(o_ref.dtype)


def swiglu_up(h, w_in, idx):
    m, d = h.shape
    f = w_in.shape[2] // 2
    bm = min(m, 1024)
    bn = _tile(f, 256)
    nj = f // bn
    return pl.pallas_call(
        _swiglu_kernel,
        grid=(m // bm, nj),
        in_specs=[
            pl.BlockSpec((bm, d), lambda i, j: (i, 0)),
            pl.BlockSpec((None, d, bn), lambda i, j: (idx, 0, j)),
            pl.BlockSpec((None, d, bn), lambda i, j: (idx, 0, j + nj)),
        ],
        out_specs=pl.BlockSpec((bm, bn), lambda i, j: (i, j)),
        out_shape=jax.ShapeDtypeStruct((m, f), MM_DTYPE),
        compiler_params=_params("parallel", "arbitrary"),
        name="swiglu_up",
    )(h, w_in, w_in)


def _mm_resid_kernel(*refs, n_lhs, scale):
    a_refs, w_refs = refs[:n_lhs], refs[n_lhs:2 * n_lhs]
    x_ref, gate_ref, o_ref = refs[2 * n_lhs:]
    y = None
    for a_ref, w_ref in zip(a_refs, w_refs):
        a = a_ref[...]
        d = jnp.dot(a, w_ref[...].astype(a.dtype), preferred_element_type=F32)
        y = d if y is None else y + d
    o_ref[...] = x_ref[...] + (scale * gate_ref[0]) * y


def mm_residual(a_list, w, idx, x, gate, scale, rows_per_seq, bm, bn):
    m, k = a_list[0].shape
    n_lhs = len(a_list)
    n = w.shape[2]
    assert w.shape[1] == n_lhs * k
    bm = _row_block(m, bm, gate, rows_per_seq)
    bn = _tile(n, bn)
    w_spec = lambda part: pl.BlockSpec((None, k, bn), lambda i, j: (idx, part, j))
    return pl.pallas_call(
        functools.partial(_mm_resid_kernel, n_lhs=n_lhs, scale=scale),
        grid=(m // bm, n // bn),
        in_specs=(
            [pl.BlockSpec((bm, k), lambda i, j: (i, 0)) for _ in range(n_lhs)]
            + [w_spec(part) for part in range(n_lhs)]
            + [pl.BlockSpec((bm, bn), lambda i, j: (i, j)), _mod_spec(gate, rows_per_seq, bm, bn, True)]
        ),
        out_specs=pl.BlockSpec((bm, bn), lambda i, j: (i, j)),
        out_shape=jax.ShapeDtypeStruct((m, n), F32),
        compiler_params=_params("parallel", "arbitrary"),
        name="mm_residual",
    )(*a_list, *([w] * n_lhs), x, gate)


def _mm_kernel(x_ref, w_ref, o_ref):
    x = x_ref[...]
    o_ref[...] = jnp.dot(x, w_ref[...].astype(x.dtype), preferred_element_type=F32)


def mm(x, w, idx, n, bm, bn):
    m, k = x.shape
    bm = min(m, bm)
    bn = _tile(n, bn)
    return pl.pallas_call(
        _mm_kernel,
        grid=(m // bm, n // bn),
        in_specs=[
            pl.BlockSpec((bm, k), lambda i, j: (i, 0)),
            pl.BlockSpec((None, k, bn), lambda i, j: (idx, 0, j)),
        ],
        out_specs=pl.BlockSpec((bm, bn), lambda i, j: (i, j)),
        out_shape=jax.ShapeDtypeStruct((m, n), F32),
        compiler_params=_params("parallel", "arbitrary"),
        name="mm",
    )(x, w)


def _head_norm_kernel(x_ref, w_ref, avg_ref, o_ref):
    x = x_ref[...]
    ms = _dot_hi(x * x, avg_ref[...])
    o_ref[...] = x * lax.rsqrt(ms + EPS) * w_ref[...]


def head_norm(proj, w_cols, ncols):
    m = proj.shape[0]
    bm = min(m, 512)
    lane_head = jnp.arange(LANES) // SWA_HEAD_DIM
    avg = (lane_head[:, None] == lane_head[None, :]).astype(F32) / SWA_HEAD_DIM
    return pl.pallas_call(
        _head_norm_kernel,
        grid=(m // bm, ncols // LANES),
        in_specs=[
            pl.BlockSpec((bm, LANES), lambda i, j: (i, j)),
            pl.BlockSpec((1, LANES), lambda i, j: (0, j)),
            pl.BlockSpec((LANES, LANES), lambda i, j: (0, 0)),
        ],
        out_specs=pl.BlockSpec((bm, LANES), lambda i, j: (i, j)),
        out_shape=jax.ShapeDtypeStruct((m, ncols), F32),
        compiler_params=_params("parallel", "parallel"),
        name="head_norm",
    )(proj, w_cols, avg)


def _swa_kernel(sink_ref, q_ref, kc_ref, kp_ref, vc_ref, vp_ref, o_ref, *, n_heads, n_kv, always_prev):
    group = n_heads // n_kv
    w = WINDOW
    qr = q_ref.shape[1]
    qi_p = lax.broadcasted_iota(jnp.int32, (qr, w), 0)
    kj_p = lax.broadcasted_iota(jnp.int32, (qr, w), 1)
    qi_c = lax.broadcasted_iota(jnp.int32, (qr, qr), 0)
    kj_c = lax.broadcasted_iota(jnp.int32, (qr, qr), 1)
    has_prev = jnp.logical_or(pl.program_id(1) > 0, always_prev)
    valid_p = jnp.logical_and(kj_p > qi_p, has_prev)
    valid_c = kj_c <= qi_c
    dist_p = (w + qi_p - kj_p).astype(F32)
    dist_c = (qi_c - kj_c).astype(F32)
    sm_scale = SWA_HEAD_DIM ** -0.5
    neg_inf = jnp.float32(-jnp.inf)
    batch = min(group, 4)
    for kvh in range(n_kv):
        ks = slice(kvh * SWA_HEAD_DIM, (kvh + 1) * SWA_HEAD_DIM)
        kc = kc_ref[0, :, ks].astype(MM_DTYPE)
        kp = kp_ref[0, :, ks].astype(MM_DTYPE)
        vc = vc_ref[0, :, ks].astype(MM_DTYPE)
        vp = vp_ref[0, :, ks].astype(MM_DTYPE)
        for g0 in range(0, group, batch):
            hs_ = [kvh * group + g for g in range(g0, g0 + batch)]
            cols = [slice(h * SWA_HEAD_DIM, (h + 1) * SWA_HEAD_DIM) for h in hs_]
            slopes = [2.0 ** (-8.0 * (h + 1) / n_heads) for h in hs_]
            sinks = [sink_ref[h] for h in hs_]
            qs = [q_ref[0, :, c].astype(MM_DTYPE) for c in cols]
            s_p = [jnp.where(valid_p, _dot_nt(q, kp) * sm_scale - sl * dist_p, neg_inf) for q, sl in zip(qs, slopes)]
            s_c = [jnp.where(valid_c, _dot_nt(q, kc) * sm_scale - sl * dist_c, neg_inf) for q, sl in zip(qs, slopes)]
            ms = [jnp.maximum(jnp.maximum(jnp.max(a, axis=-1, keepdims=True), jnp.max(b, axis=-1, keepdims=True)), sk)
                  for a, b, sk in zip(s_p, s_c, sinks)]
            p_p = [jnp.exp(a - m) for a, m in zip(s_p, ms)]
            p_c = [jnp.exp(b - m) for b, m in zip(s_c, ms)]
            den = [jnp.sum(a, axis=-1, keepdims=True) + jnp.sum(b, axis=-1, keepdims=True) + jnp.exp(sk - m)
                   for a, b, sk, m in zip(p_p, p_c, sinks, ms)]
            os_ = [_dot(a, vp) + _dot(b, vc) for a, b in zip(p_p, p_c)]
            for c, o, d in zip(cols, os_, den):
                o_ref[0, :, c] = (o / d).astype(o_ref.dtype)


def swa_attention(sinks, q_arr, kc_arr, kp_arr, vc_arr, vp_arr, *, n_heads, n_kv, q_col, kc_col, kp_col,
                  vc_col, vp_col, prev_is_cache):
    b, l, _ = q_arr.shape
    qr = min(l, WINDOW)
    assert l % qr == 0 and (prev_is_cache or qr == WINDOW)
    nb = l // qr
    qw = n_heads * SWA_HEAD_DIM
    kw = n_kv * SWA_HEAD_DIM
    if prev_is_cache:
        assert nb == 1 and kp_arr.shape[1] == WINDOW
        prev_map = lambda col: (lambda i, n: (i, 0, col))
    else:
        prev_map = lambda col: (lambda i, n: (i, jnp.maximum(n - 1, 0), col))
    return pl.pallas_call(
        functools.partial(_swa_kernel, n_heads=n_heads, n_kv=n_kv, always_prev=prev_is_cache),
        grid=(b, nb),
        in_specs=[
            pl.BlockSpec(memory_space=pltpu.SMEM),
            pl.BlockSpec((1, qr, qw), lambda i, n: (i, n, q_col)),
            pl.BlockSpec((1, qr, kw), lambda i, n: (i, n, kc_col)),
            pl.BlockSpec((1, WINDOW, kw), prev_map(kp_col)),
            pl.BlockSpec((1, qr, kw), lambda i, n: (i, n, vc_col)),
            pl.BlockSpec((1, WINDOW, kw), prev_map(vp_col)),
        ],
        out_specs=pl.BlockSpec((1, qr, qw), lambda i, n: (i, n, 0)),
        out_shape=jax.ShapeDtypeStruct((b, l, qw), MM_DTYPE),
        compiler_params=_params("parallel", "arbitrary"),
        name="swa_attention",
    )(sinks, q_arr, kc_arr, kp_arr, vc_arr, vp_arr)


def _conv_silu_kernel(xp_ref, w_ref, b_ref, o_ref):
    l = o_ref.shape[1]
    y = sum(xp_ref[0, i:i + l, :] * w_ref[i:i + 1, :] for i in range(CONV_W)) + b_ref[...]
    o_ref[0] = _silu(y)


def conv_silu(xp, w, b):
    bsz, lp, c = xp.shape
    l = lp - (CONV_W - 1)
    bc = _tile(c, max(512, (512 * 2048) // lp // LANES * LANES))
    return pl.pallas_call(
        _conv_silu_kernel,
        grid=(bsz, c // bc),
        in_specs=[
            pl.BlockSpec((1, lp, bc), lambda i, j: (i, 0, j)),
            pl.BlockSpec((CONV_W, bc), lambda i, j: (0, j)),
            pl.BlockSpec((1, bc), lambda i, j: (0, j)),
        ],
        out_specs=pl.BlockSpec((1, l, bc), lambda i, j: (i, 0, j)),
        out_shape=jax.ShapeDtypeStruct((bsz, l, c), F32),
        compiler_params=_params("parallel", "parallel"),
        name="conv_silu",
    )(xp, w, b.reshape(1, c))


def _conv_silu_long_kernel(x_ref, buf_ref, w_ref, b_ref, o_ref, head_scr):
    x = x_ref[0]
    acc = x * w_ref[CONV_W - 1:CONV_W, :] + b_ref[...]
    for s in range(1, CONV_W):
        acc = acc + pltpu.roll(x, s, 0) * w_ref[CONV_W - 1 - s:CONV_W - s, :]
    o_ref[0] = _silu(acc)
    lo = SUBLANES - (CONV_W - 1)
    head_scr[lo:SUBLANES, :] = buf_ref[0]
    head_scr[SUBLANES:2 * SUBLANES, :] = x[0:SUBLANES]
    yh = sum(head_scr[lo + i:lo + i + SUBLANES, :] * w_ref[i:i + 1, :] for i in range(CONV_W)) + b_ref[...]
    o_ref[0, 0:SUBLANES, :] = _silu(yh)


def conv_silu_long(x_arr, col0, c, buf, w, b):
    bsz, l, _ = x_arr.shape
    bc = _tile(c, 512)
    assert col0 % bc == 0 and l % SUBLANES == 0
    j0 = col0 // bc
    return pl.pallas_call(
        _conv_silu_long_kernel,
        grid=(bsz, c // bc),
        in_specs=[
            pl.BlockSpec((1, l, bc), lambda i, j: (i, 0, j0 + j)),
            pl.BlockSpec((1, CONV_W - 1, bc), lambda i, j: (i, 0, j)),
            pl.BlockSpec((CONV_W, bc), lambda i, j: (0, j)),
            pl.BlockSpec((1, bc), lambda i, j: (0, j)),
        ],
        out_specs=pl.BlockSpec((1, l, bc), lambda i, j: (i, 0, j)),
        out_shape=jax.ShapeDtypeStruct((bsz, l, c), F32),
        scratch_shapes=[pltpu.VMEM((2 * SUBLANES, bc), F32)],
        compiler_params=_params("parallel", "parallel"),
        name="conv_silu_long",
    )(x_arr, buf, w, b.reshape(1, c))


def _unit_lower_inverses(a_list, r, c):
    ri = lax.broadcasted_iota(jnp.int32, (r, r), 0)
    ci = lax.broadcasted_iota(jnp.int32, (r, r), 1)
    eye = (ri == ci).astype(F32)

    def same_block(bits):
        return lax.shift_right_logical(ri, bits) == lax.shift_right_logical(ci, bits)

    n1 = [jnp.where(same_block(3), a, 0.0) for a in a_list]
    n2 = [_dot3(n, n) for n in n1]
    n4 = [_dot3(n, n) for n in n2]
    t = [_dot3(eye - a, eye + b) for a, b in zip(n1, n2)]
    t = [_dot3(a, eye + b) for a, b in zip(t, n4)]
    bits = 3
    while (1 << bits) < c:
        pair = jnp.logical_and(same_block(bits + 1), jnp.logical_not(same_block(bits)))
        left = [_dot3(ti, jnp.where(pair, a, 0.0)) for ti, a in zip(t, a_list)]
        t = [ti - _dot3(li, ti) for ti, li in zip(t, left)]
        bits += 1
    return t


def _gdn_prep_kernel(alog_ref, dtb_ref, q_ref, k_ref, v_ref, ab_ref, w_ref, u0_ref, qe_ref, kw_ref, qk_ref, e_ref,
                     *, c, hb, n_heads, n_valid, lp):
    r = q_ref.shape[1]
    t = pl.program_id(1)
    h0 = pl.program_id(2) * hb
    bits = c.bit_length() - 1
    ri = lax.broadcasted_iota(jnp.int32, (r, r), 0)
    ci = lax.broadcasted_iota(jnp.int32, (r, r), 1)
    same = lax.shift_right_logical(ri, bits) == lax.shift_right_logical(ci, bits)
    eye = ri == ci
    lower = jnp.logical_and(ri >= ci, same)
    strict = jnp.logical_and(ri > ci, same)
    upper = jnp.logical_and(ri <= ci, same)
    lane = lax.broadcasted_iota(jnp.int32, (r, LANES), 1)
    row = lax.broadcasted_iota(jnp.int32, (r, 1), 0)
    ab = ab_ref[0]
    heads = range(hb)
    sls = [slice(hh * GDN_DK, (hh + 1) * GDN_DK) for hh in heads]
    ks, betas, a_mats = [], [], []
    for hh in heads:
        h = h0 + hh
        q = q_ref[0, :, sls[hh]]
        k = k_ref[0, :, sls[hh]]
        q = q * lax.rsqrt(jnp.sum(q * q, axis=-1, keepdims=True) + EPS) * (GDN_DK ** -0.5)
        k = k * lax.rsqrt(jnp.sum(k * k, axis=-1, keepdims=True) + EPS)
        a_col = jnp.sum(jnp.where(lane == h, ab, 0.0), axis=-1, keepdims=True)
        b_col = jnp.sum(jnp.where(lane == h + n_heads, ab, 0.0), axis=-1, keepdims=True)
        g_col = -jnp.exp(alog_ref[h]) * _softplus(a_col + dtb_ref[h])
        beta = jax.nn.sigmoid(b_col)
        if n_valid < lp:
            live = (row + t * r) < n_valid
            g_col = jnp.where(live, g_col, 0.0)
            beta = jnp.where(live, beta, 0.0)
        g_bc = jnp.broadcast_to(g_col, (r, r))
        g_row = jnp.sum(jnp.where(eye, g_bc, 0.0), axis=0, keepdims=True)
        gc_row = jnp.sum(jnp.where(upper, g_bc, 0.0), axis=0, keepdims=True)
        gc_col = jnp.sum(jnp.where(lower, jnp.broadcast_to(g_row, (r, r)), 0.0), axis=1, keepdims=True)
        decay = jnp.where(lower, jnp.exp(gc_col - gc_row), 0.0)
        eg = jnp.exp(gc_col)
        qe_ref[0, :, sls[hh]] = (q * eg).astype(qe_ref.dtype)
        qk = _dot_nt(q, k) * decay
        qkc = qk[:, 0:c]
        g_end = jnp.zeros((r, 1), F32)
        for j in range(r // c):
            if j:
                qkc = qkc + qk[:, j * c:(j + 1) * c]
            ge = gc_col[j * c + c - 1:(j + 1) * c, :]
            g_end = jnp.where(jnp.logical_and(row >= j * c, row < (j + 1) * c), ge, g_end)
            e_ref[0, j * SUBLANES:(j + 1) * SUBLANES, sls[hh]] = jnp.broadcast_to(jnp.exp(ge),
                                                                                 (SUBLANES, GDN_DV))
        qk_ref[0, :, hh * c:(hh + 1) * c] = qkc.astype(qk_ref.dtype)
        kw_ref[0, :, sls[hh]] = (k * jnp.exp(g_end - gc_col)).astype(kw_ref.dtype)
        a_mats.append(jnp.where(strict, _dot_nt(k, k) * beta * decay, 0.0))
        ks.append(k * (beta * eg))
        betas.append(beta)
    t_invs = _unit_lower_inverses(a_mats, r, c)
    ws = [_dot3(ti, kb) for ti, kb in zip(t_invs, ks)]
    u0s = [_dot3(ti, v_ref[0, :, sls[hh]] * betas[hh]) for hh, ti in zip(heads, t_invs)]
    for hh in heads:
        w_ref[0, :, sls[hh]] = ws[hh].astype(w_ref.dtype)
        u0_ref[0, :, sls[hh]] = u0s[hh]


def _gdn_scan_kernel(w_ref, u0_ref, qe_ref, kw_ref, qk_ref, e_ref, gate_ref, nw_ref, s0_ref, o_ref, s_ref,
                     *, c, hb):
    lp = w_ref.shape[1]
    for hh in range(hb):
        s_ref[0, hh] = s0_ref[0, hh]

    def chunk(t, carry):
        rows = pl.ds(pl.multiple_of(t * c, c), c)
        erow = pl.ds(pl.multiple_of(t * SUBLANES, SUBLANES), 1)
        heads = range(hb)
        sls = [slice(hh * GDN_DV, (hh + 1) * GDN_DV) for hh in heads]
        ss = [s_ref[0, hh] for hh in heads]
        us = [u0_ref[0, rows, sls[hh]] - _dot(w_ref[0, rows, sls[hh]], ss[hh]) for hh in heads]
        for hh in heads:
            s_ref[0, hh] = e_ref[0, erow, sls[hh]] * ss[hh] + _dot_tn(kw_ref[0, rows, sls[hh]], us[hh])
        os_ = [_dot(qe_ref[0, rows, sls[hh]], ss[hh]) + _dot(qk_ref[0, rows, hh * c:(hh + 1) * c], us[hh])
               for hh in heads]
        for hh in heads:
            o = os_[hh]
            o = o * lax.rsqrt(jnp.mean(o * o, axis=-1, keepdims=True) + EPS) * nw_ref[...]
            o_ref[0, rows, sls[hh]] = (o * _silu(gate_ref[0, rows, sls[hh]])).astype(o_ref.dtype)
        return carry

    lax.fori_loop(0, lp // c, chunk, 0)


def gated_deltanet(a_log, dt_bias, qkv, tail, norm_w, s0, *, n_valid):
    bsz, lp, _ = qkv.shape
    n_heads = s0.shape[1]
    c = min(GDN_CHUNK, lp)
    r = min(2 * GDN_CHUNK, lp)
    full = c == GDN_CHUNK
    hb1 = min(n_heads, 8) if full else n_heads
    hb2 = min(n_heads, 4) if full else n_heads
    nhb = n_heads // hb1
    hd = n_heads * GDN_DK
    ab_col = hd // LANES
    blk = lambda off: pl.BlockSpec((1, r, hb1 * GDN_DK), lambda i, t, h: (i, t, off + h))
    e_rows = (r // c) * SUBLANES
    w, u0, qe, kw, qk, e = pl.pallas_call(
        functools.partial(_gdn_prep_kernel, c=c, hb=hb1, n_heads=n_heads, n_valid=n_valid, lp=lp),
        grid=(bsz, lp // r, nhb),
        in_specs=[
            pl.BlockSpec(memory_space=pltpu.SMEM),
            pl.BlockSpec(memory_space=pltpu.SMEM),
            blk(0), blk(nhb), blk(2 * nhb),
            pl.BlockSpec((1, r, LANES), lambda i, t, h: (i, t, ab_col)),
        ],
        out_specs=[
            blk(0), blk(0), blk(0), blk(0),
            pl.BlockSpec((1, r, hb1 * c), lambda i, t, h: (i, t, h)),
            pl.BlockSpec((1, e_rows, hb1 * GDN_DV), lambda i, t, h: (i, t, h)),
        ],
        out_shape=[
            jax.ShapeDtypeStruct((bsz, lp, hd), MM_DTYPE),
            jax.ShapeDtypeStruct((bsz, lp, hd), F32),
            jax.ShapeDtypeStruct((bsz, lp, hd), MM_DTYPE),
            jax.ShapeDtypeStruct((bsz, lp, hd), MM_DTYPE),
            jax.ShapeDtypeStruct((bsz, lp, n_heads * c), MM_DTYPE),
            jax.ShapeDtypeStruct((bsz, (lp // c) * SUBLANES, hd), F32),
        ],
        compiler_params=_params("parallel", "parallel", "parallel"),
        name="gdn_prep",
    )(a_log, dt_bias, qkv, qkv, qkv, tail)
    blk2 = lambda width: pl.BlockSpec((1, lp, hb2 * width), lambda i, h: (i, 0, h))
    return pl.pallas_call(
        functools.partial(_gdn_scan_kernel, c=c, hb=hb2),
        grid=(bsz, n_heads // hb2),
        in_specs=[
            blk2(GDN_DK), blk2(GDN_DV), blk2(GDN_DK), blk2(GDN_DK), blk2(c),
            pl.BlockSpec((1, (lp // c) * SUBLANES, hb2 * GDN_DV), lambda i, h: (i, 0, h)),
            blk2(GDN_DV),
            pl.BlockSpec((1, GDN_DV), lambda i, h: (0, 0)),
            pl.BlockSpec((1, hb2, GDN_DK, GDN_DV), lambda i, h: (i, h, 0, 0)),
        ],
        out_specs=[
            blk2(GDN_DV),
            pl.BlockSpec((1, hb2, GDN_DK, GDN_DV), lambda i, h: (i, h, 0, 0)),
        ],
        out_shape=[
            jax.ShapeDtypeStruct((bsz, lp, n_heads * GDN_DV), MM_DTYPE),
            jax.ShapeDtypeStruct(s0.shape, F32),
        ],
        compiler_params=_params("parallel", "parallel"),
        name="gdn_scan",
    )(w, u0, qe, kw, qk, e, tail, norm_w.reshape(1, GDN_DV), s0)


def _ssd_kernel(x_ref, b_ref, c_ref, z_ref, dt_ref, alog_ref, dtb_ref, d_ref, nw_ref, s0_ref, y_ref, s_ref,
                y_scr, s_scr, *, hpg, lp, n_valid):
    g = pl.program_id(1)
    t = pl.program_id(2)
    c = x_ref.shape[1]
    p = SSD_HEAD_DIM

    @pl.when(t == 0)
    def _():
        s_scr[...] = s0_ref[0, 0].T

    ri = lax.broadcasted_iota(jnp.int32, (c, c), 0)
    ci = lax.broadcasted_iota(jnp.int32, (c, c), 1)
    lower = ri >= ci
    tri = lower.astype(F32)
    hl = lax.broadcasted_iota(jnp.int32, (LANES, LANES), 0)
    zl = lax.broadcasted_iota(jnp.int32, (LANES, LANES), 1)
    sel = jnp.logical_and(hl == g * hpg + zl, zl < hpg).astype(F32)

    xs = x_ref[0]
    bm = b_ref[0]
    cm = c_ref[0]
    dt = _softplus(dt_ref[0] + dtb_ref[...])
    if n_valid < lp:
        live = (lax.broadcasted_iota(jnp.int32, (c, 1), 0) + t * c) < n_valid
        dt = jnp.where(live, dt, 0.0)
    gcum = _dot_hi(tri, dt * (-jnp.exp(alog_ref[...])))
    gcg = _dot_hi(gcum, sel)
    gcg_t = gcg.T
    eh = lax.broadcasted_iota(jnp.int32, (LANES, hpg * p), 0)
    ec = lax.shift_right_logical(lax.broadcasted_iota(jnp.int32, (LANES, hpg * p), 1), p.bit_length() - 1)
    spread = (eh == g * hpg + ec).astype(MM_DTYPE)
    per_head = jnp.concatenate([dt, jnp.exp(gcum), jnp.exp(gcum[c - 1:c, :] - gcum),
                                jnp.broadcast_to(d_ref[...], (SUBLANES, LANES))], axis=0)
    hi, lo = _hi_lo(per_head)
    per_chan = (jnp.dot(hi, spread, preferred_element_type=F32)
                + jnp.dot(lo, spread, preferred_element_type=F32))
    dt_x = per_chan[0:c]
    eg_x = per_chan[c:2 * c]
    ew_x = per_chan[2 * c:3 * c]
    d_x = per_chan[3 * c:3 * c + 1]
    cb = _dot_nt(cm, bm)
    s = s_scr[...]
    xdt = xs * dt_x
    y_scr[...] = _dot(cm, s) * eg_x + d_x * xs
    for z in range(hpg):
        sl = slice(z * p, (z + 1) * p)
        decay = jnp.where(lower, jnp.exp(gcg[:, z:z + 1] - gcg_t[z:z + 1, :]), 0.0)
        y_scr[:, sl] = y_scr[:, sl] + _dot(cb * decay, xdt[:, sl])
    s_new = eg_x[c - 1:c, :] * s + _dot_tn(bm, xdt * ew_x)
    s_scr[...] = s_new

    @pl.when(t == pl.num_programs(2) - 1)
    def _():
        s_ref[0, 0] = s_new.T

    y = y_scr[...] * _silu(z_ref[0])
    y = y * lax.rsqrt(jnp.mean(y * y, axis=-1, keepdims=True) + EPS) * nw_ref[...]
    y_ref[0] = y.astype(y_ref.dtype)


def ssd_mixer(xbc, z_arr, dt_arr, a_log, dt_bias, d_skip, norm_w, s0_g, *, n_groups, z_col, dt_col, n_valid):
    bsz, lp, _ = xbc.shape
    n_heads = a_log.shape[0]
    assert n_heads == LANES
    hpg = n_heads // n_groups
    gw = hpg * SSD_HEAD_DIM
    d_inner = n_heads * SSD_HEAD_DIM
    c = min(SSD_CHUNK, lp)
    nc = lp // c
    assert gw % LANES == 0 and d_inner % gw == 0
    b_col0 = d_inner // SSD_STATE
    vec = lambda a: a.reshape(1, n_heads)
    return pl.pallas_call(
        functools.partial(_ssd_kernel, hpg=hpg, lp=lp, n_valid=n_valid),
        grid=(bsz, n_groups, nc),
        in_specs=[
            pl.BlockSpec((1, c, gw), lambda i, g, t: (i, t, g)),
            pl.BlockSpec((1, c, SSD_STATE), lambda i, g, t: (i, t, b_col0 + g)),
            pl.BlockSpec((1, c, SSD_STATE), lambda i, g, t: (i, t, b_col0 + n_groups + g)),
            pl.BlockSpec((1, c, gw), lambda i, g, t: (i, t, z_col + g)),
            pl.BlockSpec((1, c, LANES), lambda i, g, t: (i, t, dt_col)),
            pl.BlockSpec((1, LANES), lambda i, g, t: (0, 0)),
            pl.BlockSpec((1, LANES), lambda i, g, t: (0, 0)),
            pl.BlockSpec((1, LANES), lambda i, g, t: (0, 0)),
            pl.BlockSpec((1, gw), lambda i, g, t: (0, g)),
            pl.BlockSpec((1, 1, gw, SSD_STATE), lambda i, g, t: (i, g, 0, 0)),
        ],
        out_specs=[
            pl.BlockSpec((1, c, gw), lambda i, g, t: (i, t, g)),
            pl.BlockSpec((1, 1, gw, SSD_STATE), lambda i, g, t: (i, g, 0, 0)),
        ],
        out_shape=[
            jax.ShapeDtypeStruct((bsz, lp, d_inner), MM_DTYPE),
            jax.ShapeDtypeStruct(s0_g.shape, F32),
        ],
        scratch_shapes=[pltpu.VMEM((c, gw), F32), pltpu.VMEM((SSD_STATE, gw), F32)],
        compiler_params=_params("parallel", "parallel", "arbitrary"),
        name="ssd_mixer",
    )(xbc, xbc, xbc, z_arr, dt_arr, vec(a_log), vec(dt_bias), vec(d_skip), norm_w.reshape(1, d_inner), s0_g)


def _pad_rows(a, lp):
    return jnp.pad(a, ((0, 0), (0, lp - a.shape[1]), (0, 0)))


def _round_up(n, mult):
    return -(-n // mult) * mult


def _even_tail_weights(ev_w_in, o_ab, n_heads_b):
    o_g = o_ab + 2 * n_heads_b
    ab = ev_w_in[:, :, o_ab:o_g]
    gate = ev_w_in[:, :, o_g:]
    width = _round_up(gate.shape[2] + LANES, 3 * LANES)
    pad = jnp.zeros(ab.shape[:2] + (width - gate.shape[2] - 2 * n_heads_b,), ev_w_in.dtype)
    return jnp.concatenate([gate, ab, pad], axis=-1)


def _trunk(x3, mod_all, p, cache):
    bsz, l, d = x3.shape
    m = bsz * l
    x = x3.reshape(m, d)
    depth = p['norm_w'].shape[0]
    n_heads_a = p['swa_sinks'].shape[1]
    n_kv = p['n_kv']
    n_heads_b = p['gdn_a_log'].shape[1]
    n_heads_c = p['ssm_a_log'].shape[1]
    d_inner = n_heads_c * SSD_HEAD_DIM
    n_groups = (p['ssm_conv_w'].shape[2] - d_inner) // (2 * SSD_STATE)
    hpg = n_heads_c // n_groups
    qa = n_heads_a * SWA_HEAD_DIM
    kv = n_kv * SWA_HEAD_DIM
    conv_b = n_heads_b * (2 * GDN_DK + GDN_DV)
    conv_c = d_inner + 2 * n_groups * SSD_STATE
    expand = cache is not None
    out = {'k': [], 'v': [], 'gdn': [], 'gdn_conv': [], 'ssm': [], 'ssm_conv': []}

    def mod_of(layer, idx):
        v = mod_all[layer, idx]
        return v.reshape(1, m, d) if expand else v.reshape(bsz, 1, d)

    o_qkv = qa + 2 * kv
    o_ab = o_qkv + conv_b
    long_seq = l % SSD_CHUNK == 0 and l % GDN_CHUNK == 0
    lp = l if long_seq else _round_up(l, SUBLANES)
    for layer in range(depth):
        mo = functools.partial(mod_of, layer)
        h = ada_norm(x, p['norm_w'][layer, 0], mo(0), mo(1), l)
        act = swiglu_up(h, p['ffn_w_in'], 2 * layer)
        x = mm_residual([act], p['ffn_w_out'], 2 * layer, x, mo(2), 0.5, l, 512, 512)
        h = ada_norm(x, p['norm_w'][layer, 1], mo(3), mo(4), l)
        if layer % 2 == 0:
            e = layer // 2
            proj = mm(h, p['ev_w_in'], e, o_ab, 1024, 512)
            proj3 = proj.reshape(bsz, l, o_ab)
            tail3 = mm(h, p['ev_w_tail'], e, p['ev_w_tail'].shape[2], 1024, 384).reshape(bsz, l, -1)
            w_cols = jnp.concatenate([jnp.tile(p['swa_q_norm'][e], n_heads_a),
                                      jnp.tile(p['swa_k_norm'][e], n_kv)]).reshape(1, qa + kv)
            qk = head_norm(proj, w_cols, qa + kv).reshape(bsz, l, qa + kv)
            k_new = qk[:, :, qa:]
            v_new = proj3[:, :, qa + kv:qa + 2 * kv]
            if cache is None:
                o_a = swa_attention(p['swa_sinks'][e], qk, qk, qk, proj3, proj3, n_heads=n_heads_a, n_kv=n_kv,
                                    q_col=0, kc_col=qa // kv, kp_col=qa // kv, vc_col=(qa + kv) // kv,
                                    vp_col=(qa + kv) // kv, prev_is_cache=False)
                wb = min(WINDOW, l)
                out['k'].append(k_new[:, l - wb:].reshape(bsz, wb, n_kv, SWA_HEAD_DIM))
                out['v'].append(v_new[:, l - wb:].reshape(bsz, wb, n_kv, SWA_HEAD_DIM))
                conv0 = jnp.zeros((bsz, CONV_W - 1, conv_b), F32)
                s0 = jnp.zeros((bsz, n_heads_b, GDN_DK, GDN_DV), F32)
            else:
                kbuf = cache['k'][e].reshape(bsz, -1, kv)
                vbuf = cache['v'][e].reshape(bsz, -1, kv)
                o_a = swa_attention(p['swa_sinks'][e], _pad_rows(qk, lp), _pad_rows(qk, lp), kbuf,
                                    _pad_rows(v_new, lp), vbuf, n_heads=n_heads_a, n_kv=n_kv,
                                    q_col=0, kc_col=qa // kv, kp_col=0, vc_col=0, vp_col=0,
                                    prev_is_cache=True)[:, :l]
                out['k'].append(jnp.concatenate([kbuf, k_new], axis=1)[:, l:].reshape(bsz, -1, n_kv, SWA_HEAD_DIM))
                out['v'].append(jnp.concatenate([vbuf, v_new], axis=1)[:, l:].reshape(bsz, -1, n_kv, SWA_HEAD_DIM))
                conv0 = cache['gdn_conv'][e]
                s0 = cache['gdn'][e]
            if long_seq:
                out['gdn_conv'].append(proj3[:, l - (CONV_W - 1):, o_qkv:o_ab])
                qkv = conv_silu_long(proj3, o_qkv, conv_b, conv0, p['gdn_conv_w'][e], p['gdn_conv_b'][e])
            else:
                xp = jnp.concatenate([conv0, proj3[:, :, o_qkv:o_ab]], axis=1)
                out['gdn_conv'].append(xp[:, xp.shape[1] - (CONV_W - 1):])
                qkv = _pad_rows(conv_silu(xp, p['gdn_conv_w'][e], p['gdn_conv_b'][e]), lp)
                tail3 = _pad_rows(tail3, lp)
            o_b, s_new = gated_deltanet(p['gdn_a_log'][e], p['gdn_dt_bias'][e], qkv, tail3, p['gdn_norm_w'][e], s0,
                                        n_valid=l)
            out['gdn'].append(s_new)
            x = mm_residual([o_a.reshape(m, -1), o_b[:, :l].reshape(m, -1)], p['ev_w_out'], e, x, mo(5), 1.0, l,
                            1024, 512)
        else:
            oi = layer // 2
            proj = mm(h, p['od_w_in'], oi, p['od_w_in'].shape[2], 1024, 640)
            proj3 = proj.reshape(bsz, l, -1)
            if cache is None:
                conv0 = jnp.zeros((bsz, CONV_W - 1, conv_c), F32)
                s0_g = jnp.zeros((bsz, n_groups, hpg * SSD_HEAD_DIM, SSD_STATE), F32)
            else:
                conv0 = cache['ssm_conv'][oi]
                s0_g = cache['ssm'][oi].reshape(bsz, n_groups, hpg * SSD_HEAD_DIM, SSD_STATE)
            o_dt = d_inner + conv_c
            if long_seq:
                out['ssm_conv'].append(proj3[:, l - (CONV_W - 1):, d_inner:o_dt])
                xbc = conv_silu_long(proj3, d_inner, conv_c, conv0, p['ssm_conv_w'][oi], p['ssm_conv_b'][oi])
                zdt = proj3
            else:
                xp = jnp.concatenate([conv0, proj3[:, :, d_inner:o_dt]], axis=1)
                out['ssm_conv'].append(xp[:, xp.shape[1] - (CONV_W - 1):])
                xbc = _pad_rows(conv_silu(xp, p['ssm_conv_w'][oi], p['ssm_conv_b'][oi]), lp)
                zdt = _pad_rows(proj3, lp)
            y, s_g = ssd_mixer(xbc, zdt, zdt, p['ssm_a_log'][oi], p['ssm_dt_bias'][oi], p['ssm_d'][oi],
                               p['ssm_norm_w'][oi], s0_g, n_groups=n_groups, z_col=0, dt_col=o_dt // LANES,
                               n_valid=l)
            out['ssm'].append(s_g.reshape(bsz, n_heads_c, SSD_HEAD_DIM, SSD_STATE))
            x = mm_residual([y[:, :l].reshape(m, d_inner)], p['od_w_out'], oi, x, mo(5), 1.0, l, 512, 512)
        h = ada_norm(x, p['norm_w'][layer, 2], mo(6), mo(7), l)
        act = swiglu_up(h, p['ffn_w_in'], 2 * layer + 1)
        x = mm_residual([act], p['ffn_w_out'], 2 * layer + 1, x, mo(8), 0.5, l, 512, 512)
    return (x.reshape(bsz, l, d), jnp.stack(out['k']), jnp.stack(out['v']), jnp.stack(out['gdn']),
            jnp.stack(out['gdn_conv']), jnp.stack(out['ssm']), jnp.stack(out['ssm_conv']))


def kernel(x_prompt, x_sample, c_prompt, c_sample, cache_swa_k, cache_swa_v, state_gdn, state_gdn_conv, state_ssm, state_ssm_conv, ada_w, ada_b, norm_w, ffn_w_in, ffn_w_out, ev_w_in, ev_w_out, swa_q_norm, swa_k_norm, swa_sinks, gdn_conv_w, gdn_conv_b, gdn_a_log, gdn_dt_bias, gdn_norm_w, od_w_in, od_w_out, ssm_conv_w, ssm_conv_b, ssm_a_log, ssm_dt_bias, ssm_d, ssm_norm_w):
    depth, d = norm_w.shape[0], norm_w.shape[2]
    n_kv = cache_swa_k.shape[3]
    n_heads_a = swa_sinks.shape[1]
    n_heads_b = gdn_a_log.shape[1]
    bp, bs = c_prompt.shape[0], c_sample.shape[0]
    r = bp + bs
    r_pad = -(-r // 16) * 16
    c_all = jnp.pad(jnp.concatenate([c_prompt, c_sample], axis=0), ((0, r_pad - r), (0, 0)))
    mod = ada_modulation(c_all, ada_w, ada_b).reshape(depth, r_pad, N_MOD, d).transpose(0, 2, 1, 3)
    mod_p = mod[:, :, :bp]
    mod_s = jnp.repeat(mod[:, :, bp:r], x_sample.shape[1], axis=2)
    o_ab = (n_heads_a + 2 * n_kv) * SWA_HEAD_DIM + n_heads_b * (2 * GDN_DK + GDN_DV)
    f2 = ffn_w_in.shape[3]
    p = {
        'norm_w': norm_w,
        'ffn_w_in': ffn_w_in.reshape(2 * depth, d, f2),
        'ffn_w_out': ffn_w_out.astype(MM_DTYPE).reshape(2 * depth, f2 // 2, d),
        'ev_w_in': ev_w_in, 'ev_w_tail': _even_tail_weights(ev_w_in, o_ab, n_heads_b),
        'ev_w_out': ev_w_out.astype(MM_DTYPE),
        'swa_q_norm': swa_q_norm, 'swa_k_norm': swa_k_norm, 'swa_sinks': swa_sinks, 'n_kv': n_kv,
        'gdn_conv_w': gdn_conv_w, 'gdn_conv_b': gdn_conv_b, 'gdn_a_log': gdn_a_log,
        'gdn_dt_bias': gdn_dt_bias, 'gdn_norm_w': gdn_norm_w,
        'od_w_in': od_w_in, 'od_w_out': od_w_out.astype(MM_DTYPE),
        'ssm_conv_w': ssm_conv_w, 'ssm_conv_b': ssm_conv_b, 'ssm_a_log': ssm_a_log,
        'ssm_dt_bias': ssm_dt_bias, 'ssm_d': ssm_d, 'ssm_norm_w': ssm_norm_w,
    }
    cache = {'k': cache_swa_k, 'v': cache_swa_v, 'gdn': state_gdn, 'gdn_conv': state_gdn_conv,
             'ssm': state_ssm, 'ssm_conv': state_ssm_conv}
    y_p, k_p, v_p, gdn_p, gdn_conv_p, ssm_p, ssm_conv_p = _trunk(x_prompt, mod_p, p, None)
    y_s, k_s, v_s, gdn_s, gdn_conv_s, ssm_s, ssm_conv_s = _trunk(x_sample, mod_s, p, cache)
    return (y_p, y_s, k_p, v_p, gdn_p, gdn_conv_p, ssm_p, ssm_conv_p,
            k_s, v_s, gdn_s, gdn_conv_s, ssm_s, ssm_conv_s)
```

```python
import functools
import math

import jax
import jax.numpy as jnp
from jax import lax
from jax.experimental import pallas as pl
from jax.experimental.pallas import tpu as pltpu

EPS = 1e-6
N_MOD = 9
SWA_HEAD_DIM = 64
WINDOW = 128
GDN_DK = 128
GDN_DV = 128
GDN_CHUNK = 64
CONV_W = 4
SSD_HEAD_DIM = 64
SSD_STATE = 128
SSD_CHUNK = 128

LANES = 128
SUBLANES = 8
VMEM_LIMIT_BYTES = 56 * 1024 * 1024

MM_DTYPE = jnp.bfloat16
HI = lax.Precision.HIGHEST
F32 = jnp.float32


def _dot(a, b):
    return jnp.dot(a.astype(MM_DTYPE), b.astype(MM_DTYPE), preferred_element_type=F32)


def _dot_nt(a, b):
    return lax.dot_general(a.astype(MM_DTYPE), b.astype(MM_DTYPE), (((1,), (1,)), ((), ())),
                           preferred_element_type=F32)


def _dot_tn(a, b):
    return lax.dot_general(a.astype(MM_DTYPE), b.astype(MM_DTYPE), (((0,), (0,)), ((), ())),
                           preferred_element_type=F32)


def _dot_hi(a, b):
    return jnp.dot(a, b, preferred_element_type=F32, precision=HI)


def _hi_lo(a):
    hi = a.astype(MM_DTYPE)
    return hi, (a - hi.astype(F32)).astype(MM_DTYPE)


def _dot3(a, b):
    ah, al = _hi_lo(a)
    bh, bl = _hi_lo(b)
    return (jnp.dot(ah, bh, preferred_element_type=F32)
            + (jnp.dot(ah, bl, preferred_element_type=F32) + jnp.dot(al, bh, preferred_element_type=F32)))


def _silu(x):
    return x * jax.nn.sigmoid(x)


def _softplus(x):
    return jnp.maximum(x, 0.0) + jnp.log1p(jnp.exp(-jnp.abs(x)))


def _params(*sem):
    return pltpu.CompilerParams(dimension_semantics=sem, vmem_limit_bytes=VMEM_LIMIT_BYTES)


def _tile(n, target):
    if n <= target:
        return n
    t = (target // LANES) * LANES
    while t > LANES and n % t:
        t -= LANES
    assert n % t == 0, (n, target)
    return t


def _ada_kernel(c_ref, w_ref, b_ref, o_ref):
    a = _silu(c_ref[...])
    o_ref[...] = _dot(a, w_ref[...]) + b_ref[...]


def ada_modulation(c_all, ada_w, ada_b):
    depth, d, n = ada_w.shape
    r = c_all.shape[0]
    bn = _tile(n, 512)
    return pl.pallas_call(
        _ada_kernel,
        grid=(depth, n // bn),
        in_specs=[
            pl.BlockSpec((r, d), lambda l, j: (0, 0)),
            pl.BlockSpec((None, d, bn), lambda l, j: (l, 0, j)),
            pl.BlockSpec((None, 1, bn), lambda l, j: (l, 0, j)),
        ],
        out_specs=pl.BlockSpec((None, r, bn), lambda l, j: (l, 0, j)),
        out_shape=jax.ShapeDtypeStruct((depth, r, n), F32),
        compiler_params=_params("parallel", "parallel"),
        name="ada_modulation",
    )(c_all, ada_w, ada_b.reshape(depth, 1, n))


def _ada_norm_kernel(x_ref, w_ref, shift_ref, scale_ref, o_ref):
    x = x_ref[...]
    h = x * lax.rsqrt(jnp.mean(x * x, axis=-1, keepdims=True) + EPS) * w_ref[...]
    o_ref[...] = (h * (1.0 + scale_ref[0]) + shift_ref[0]).astype(o_ref.dtype)


def _mod_spec(mod, rows_per_seq, bm, bn, two_d_grid):
    g, r, _ = mod.shape
    if r == 1:
        assert rows_per_seq % bm == 0
        per = rows_per_seq // bm
        if two_d_grid:
            return pl.BlockSpec((1, 1, bn), lambda i, j: (i // per, 0, j))
        return pl.BlockSpec((1, 1, bn), lambda i: (i // per, 0, 0))
    assert g == 1 and r == bm
    if two_d_grid:
        return pl.BlockSpec((1, r, bn), lambda i, j: (0, 0, j))
    return pl.BlockSpec((1, r, bn), lambda i: (0, 0, 0))


def _row_block(m, target, mod, rows_per_seq):
    return min(m, target, rows_per_seq) if mod.shape[1] == 1 else min(m, target)


def ada_norm(x, w, shift, scale, rows_per_seq):
    m, d = x.shape
    bm = _row_block(m, 256, shift, rows_per_seq)
    return pl.pallas_call(
        _ada_norm_kernel,
        grid=(m // bm,),
        in_specs=[
            pl.BlockSpec((bm, d), lambda i: (i, 0)),
            pl.BlockSpec((1, d), lambda i: (0, 0)),
            _mod_spec(shift, rows_per_seq, bm, d, False),
            _mod_spec(scale, rows_per_seq, bm, d, False),
        ],
        out_specs=pl.BlockSpec((bm, d), lambda i: (i, 0)),
        out_shape=jax.ShapeDtypeStruct((m, d), MM_DTYPE),
        compiler_params=_params("parallel"),
        name="ada_norm",
    )(x, w.reshape(1, d), shift, scale)


def _swiglu_kernel(x_ref, wg_ref, wu_ref, o_ref):
    x = x_ref[...]
    g = jnp.dot(x, wg_ref[...].astype(x.dtype), preferred_element_type=F32)
    u = jnp.dot(x, wu_ref[...].astype(x.dtype), preferred_element_type=F32)
    o_ref[...] = (_silu(g) * u).astype(o_ref.dtype)


def swiglu_up(h, w_in, idx):
    m, d = h.shape
    f = w_in.shape[2] // 2
    bm = min(m, 1024)
    bn = _tile(f, 256)
    nj = f // bn
    return pl.pallas_call(
        _swiglu_kernel,
        grid=(m // bm, nj),
        in_specs=[
            pl.BlockSpec((bm, d), lambda i, j: (i, 0)),
            pl.BlockSpec((None, d, bn), lambda i, j: (idx, 0, j)),
            pl.BlockSpec((None, d, bn), lambda i, j: (idx, 0, j + nj)),
        ],
        out_specs=pl.BlockSpec((bm, bn), lambda i, j: (i, j)),
        out_shape=jax.ShapeDtypeStruct((m, f), MM_DTYPE),
        compiler_params=_params("parallel", "arbitrary"),
        name="swiglu_up",
    )(h, w_in, w_in)


def _mm_resid_kernel(*refs, n_lhs, scale):
    a_refs, w_refs = refs[:n_lhs], refs[n_lhs:2 * n_lhs]
    x_ref, gate_ref, o_ref = refs[2 * n_lhs:]
    y = None
    for a_ref, w_ref in zip(a_refs, w_refs):
        a = a_ref[...]
        d = jnp.dot(a, w_ref[...].astype(a.dtype), preferred_element_type=F32)
        y = d if y is None else y + d
    o_ref[...] = x_ref[...] + (scale * gate_ref[0]) * y


def mm_residual(a_list, w, idx, x, gate, scale, rows_per_seq, bm, bn):
    m, k = a_list[0].shape
    n_lhs = len(a_list)
    n = w.shape[2]
    assert w.shape[1] == n_lhs * k
    bm = _row_block(m, bm, gate, rows_per_seq)
    bn = _tile(n, bn)
    w_spec = lambda part: pl.BlockSpec((None, k, bn), lambda i, j: (idx, part, j))
    return pl.pallas_call(
        functools.partial(_mm_resid_kernel, n_lhs=n_lhs, scale=scale),
        grid=(m // bm, n // bn),
        in_specs=(
            [pl.BlockSpec((bm, k), lambda i, j: (i, 0)) for _ in range(n_lhs)]
            + [w_spec(part) for part in range(n_lhs)]
            + [pl.BlockSpec((bm, bn), lambda i, j: (i, j)), _mod_spec(gate, rows_per_seq, bm, bn, True)]
        ),
        out_specs=pl.BlockSpec((bm, bn), lambda i, j: (i, j)),
        out_shape=jax.ShapeDtypeStruct((m, n), F32),
        compiler_params=_params("parallel", "arbitrary"),
        name="mm_residual",
    )(*a_list, *([w] * n_lhs), x, gate)


def _mm_kernel(x_ref, w_ref, o_ref):
    x = x_ref[...]
    o_ref[...] = jnp.dot(x, w_ref[...].astype(x.dtype), preferred_element_type=F32)


def mm(x, w, idx, n, bm, bn, col0=0):
    m, k = x.shape
    bm = min(m, bm)
    bn = _tile(n, bn)
    assert col0 % bn == 0
    j0 = col0 // bn
    return pl.pallas_call(
        _mm_kernel,
        grid=(m // bm, n // bn),
        in_specs=[
            pl.BlockSpec((bm, k), lambda i, j: (i, 0)),
            pl.BlockSpec((None, k, bn), lambda i, j: (idx, 0, j0 + j)),
        ],
        out_specs=pl.BlockSpec((bm, bn), lambda i, j: (i, j)),
        out_shape=jax.ShapeDtypeStruct((m, n), F32),
        compiler_params=_params("parallel", "arbitrary"),
        name="mm",
    )(x, w)


def _head_norm_kernel(x_ref, w_ref, avg_ref, o_ref):
    x = x_ref[...]
    ms = _dot_hi(x * x, avg_ref[...])
    o_ref[...] = x * lax.rsqrt(ms + EPS) * w_ref[...]


def head_norm(proj, w_cols, ncols):
    m = proj.shape[0]
    bm = min(m, 512)
    lane_head = jnp.arange(LANES) // SWA_HEAD_DIM
    avg = (lane_head[:, None] == lane_head[None, :]).astype(F32) / SWA_HEAD_DIM
    return pl.pallas_call(
        _head_norm_kernel,
        grid=(m // bm, ncols // LANES),
        in_specs=[
            pl.BlockSpec((bm, LANES), lambda i, j: (i, j)),
            pl.BlockSpec((1, LANES), lambda i, j: (0, j)),
            pl.BlockSpec((LANES, LANES), lambda i, j: (0, 0)),
        ],
        out_specs=pl.BlockSpec((bm, LANES), lambda i, j: (i, j)),
        out_shape=jax.ShapeDtypeStruct((m, ncols), F32),
        compiler_params=_params("parallel", "parallel"),
        name="head_norm",
    )(proj, w_cols, avg)


def _swa_kernel(sink_ref, q_ref, kc_ref, kp_ref, vc_ref, vp_ref, o_ref, *, n_heads, n_kv, always_prev):
    group = n_heads // n_kv
    w = WINDOW
    qr = q_ref.shape[1]
    qi_p = lax.broadcasted_iota(jnp.int32, (qr, w), 0)
    kj_p = lax.broadcasted_iota(jnp.int32, (qr, w), 1)
    qi_c = lax.broadcasted_iota(jnp.int32, (qr, qr), 0)
    kj_c = lax.broadcasted_iota(jnp.int32, (qr, qr), 1)
    has_prev = jnp.logical_or(pl.program_id(1) > 0, always_prev)
    valid_p = jnp.logical_and(kj_p > qi_p, has_prev)
    valid_c = kj_c <= qi_c
    dist_p = (w + qi_p - kj_p).astype(F32)
    dist_c = (qi_c - kj_c).astype(F32)
    sm_scale = SWA_HEAD_DIM ** -0.5
    neg_inf = jnp.float32(-jnp.inf)
    batch = min(group, 4)
    for kvh in range(n_kv):
        ks = slice(kvh * SWA_HEAD_DIM, (kvh + 1) * SWA_HEAD_DIM)
        kc = kc_ref[0, :, ks].astype(MM_DTYPE)
        kp = kp_ref[0, :, ks].astype(MM_DTYPE)
        vc = vc_ref[0, :, ks].astype(MM_DTYPE)
        vp = vp_ref[0, :, ks].astype(MM_DTYPE)
        for g0 in range(0, group, batch):
            hs_ = [kvh * group + g for g in range(g0, g0 + batch)]
            cols = [slice(h * SWA_HEAD_DIM, (h + 1) * SWA_HEAD_DIM) for h in hs_]
            slopes = [2.0 ** (-8.0 * (h + 1) / n_heads) for h in hs_]
            sinks = [sink_ref[h] for h in hs_]
            qs = [q_ref[0, :, c].astype(MM_DTYPE) for c in cols]
            s_p = [jnp.where(valid_p, _dot_nt(q, kp) * sm_scale - sl * dist_p, neg_inf) for q, sl in zip(qs, slopes)]
            s_c = [jnp.where(valid_c, _dot_nt(q, kc) * sm_scale - sl * dist_c, neg_inf) for q, sl in zip(qs, slopes)]
            ms = [jnp.maximum(jnp.maximum(jnp.max(a, axis=-1, keepdims=True), jnp.max(b, axis=-1, keepdims=True)), sk)
                  for a, b, sk in zip(s_p, s_c, sinks)]
            p_p = [jnp.exp(a - m) for a, m in zip(s_p, ms)]
            p_c = [jnp.exp(b - m) for b, m in zip(s_c, ms)]
            den = [jnp.sum(a, axis=-1, keepdims=True) + jnp.sum(b, axis=-1, keepdims=True) + jnp.exp(sk - m)
                   for a, b, sk, m in zip(p_p, p_c, sinks, ms)]
            os_ = [_dot(a, vp) + _dot(b, vc) for a, b in zip(p_p, p_c)]
            for c, o, d in zip(cols, os_, den):
                o_ref[0, :, c] = (o / d).astype(o_ref.dtype)


def swa_attention(sinks, q_arr, kc_arr, kp_arr, vc_arr, vp_arr, *, n_heads, n_kv, q_col, kc_col, kp_col,
                  vc_col, vp_col, prev_is_cache):
    b, l, _ = q_arr.shape
    qr = min(l, WINDOW)
    assert l % qr == 0 and (prev_is_cache or qr == WINDOW)
    nb = l // qr
    qw = n_heads * SWA_HEAD_DIM
    kw = n_kv * SWA_HEAD_DIM
    if prev_is_cache:
        assert nb == 1 and kp_arr.shape[1] == WINDOW
        prev_map = lambda col: (lambda i, n: (i, 0, col))
    else:
        prev_map = lambda col: (lambda i, n: (i, jnp.maximum(n - 1, 0), col))
    return pl.pallas_call(
        functools.partial(_swa_kernel, n_heads=n_heads, n_kv=n_kv, always_prev=prev_is_cache),
        grid=(b, nb),
        in_specs=[
            pl.BlockSpec(memory_space=pltpu.SMEM),
            pl.BlockSpec((1, qr, qw), lambda i, n: (i, n, q_col)),
            pl.BlockSpec((1, qr, kw), lambda i, n: (i, n, kc_col)),
            pl.BlockSpec((1, WINDOW, kw), prev_map(kp_col)),
            pl.BlockSpec((1, qr, kw), lambda i, n: (i, n, vc_col)),
            pl.BlockSpec((1, WINDOW, kw), prev_map(vp_col)),
        ],
        out_specs=pl.BlockSpec((1, qr, qw), lambda i, n: (i, n, 0)),
        out_shape=jax.ShapeDtypeStruct((b, l, qw), MM_DTYPE),
        compiler_params=_params("parallel", "arbitrary"),
        name="swa_attention",
    )(sinks, q_arr, kc_arr, kp_arr, vc_arr, vp_arr)


def _conv_silu_kernel(xp_ref, w_ref, b_ref, o_ref):
    l = o_ref.shape[1]
    y = sum(xp_ref[0, i:i + l, :] * w_ref[i:i + 1, :] for i in range(CONV_W)) + b_ref[...]
    o_ref[0] = _silu(y)


def conv_silu(xp, w, b):
    bsz, lp, c = xp.shape
    l = lp - (CONV_W - 1)
    bc = _tile(c, max(512, (512 * 2048) // lp // LANES * LANES))
    return pl.pallas_call(
        _conv_silu_kernel,
        grid=(bsz, c // bc),
        in_specs=[
            pl.BlockSpec((1, lp, bc), lambda i, j: (i, 0, j)),
            pl.BlockSpec((CONV_W, bc), lambda i, j: (0, j)),
            pl.BlockSpec((1, bc), lambda i, j: (0, j)),
        ],
        out_specs=pl.BlockSpec((1, l, bc), lambda i, j: (i, 0, j)),
        out_shape=jax.ShapeDtypeStruct((bsz, l, c), F32),
        compiler_params=_params("parallel", "parallel"),
        name="conv_silu",
    )(xp, w, b.reshape(1, c))


def _conv_silu_long_kernel(x_ref, buf_ref, w_ref, b_ref, o_ref, head_scr):
    x = x_ref[0]
    acc = x * w_ref[CONV_W - 1:CONV_W, :] + b_ref[...]
    for s in range(1, CONV_W):
        acc = acc + pltpu.roll(x, s, 0) * w_ref[CONV_W - 1 - s:CONV_W - s, :]
    o_ref[0] = _silu(acc)
    lo = SUBLANES - (CONV_W - 1)
    head_scr[lo:SUBLANES, :] = buf_ref[0]
    head_scr[SUBLANES:2 * SUBLANES, :] = x[0:SUBLANES]
    yh = sum(head_scr[lo + i:lo + i + SUBLANES, :] * w_ref[i:i + 1, :] for i in range(CONV_W)) + b_ref[...]
    o_ref[0, 0:SUBLANES, :] = _silu(yh)


def conv_silu_long(x_arr, col0, c, buf, w, b):
    bsz, l, _ = x_arr.shape
    bc = _tile(c, 512)
    assert col0 % bc == 0 and l % SUBLANES == 0
    j0 = col0 // bc
    return pl.pallas_call(
        _conv_silu_long_kernel,
        grid=(bsz, c // bc),
        in_specs=[
            pl.BlockSpec((1, l, bc), lambda i, j: (i, 0, j0 + j)),
            pl.BlockSpec((1, CONV_W - 1, bc), lambda i, j: (i, 0, j)),
            pl.BlockSpec((CONV_W, bc), lambda i, j: (0, j)),
            pl.BlockSpec((1, bc), lambda i, j: (0, j)),
        ],
        out_specs=pl.BlockSpec((1, l, bc), lambda i, j: (i, 0, j)),
        out_shape=jax.ShapeDtypeStruct((bsz, l, c), F32),
        scratch_shapes=[pltpu.VMEM((2 * SUBLANES, bc), F32)],
        compiler_params=_params("parallel", "parallel"),
        name="conv_silu_long",
    )(x_arr, buf, w, b.reshape(1, c))


def _unit_lower_inverses(a_list, r, c):
    ri = lax.broadcasted_iota(jnp.int32, (r, r), 0)
    ci = lax.broadcasted_iota(jnp.int32, (r, r), 1)
    eye = (ri == ci).astype(F32)

    def same_block(bits):
        return lax.shift_right_logical(ri, bits) == lax.shift_right_logical(ci, bits)

    n1 = [jnp.where(same_block(3), a, 0.0) for a in a_list]
    n2 = [_dot3(n, n) for n in n1]
    n4 = [_dot3(n, n) for n in n2]
    t = [_dot3(eye - a, eye + b) for a, b in zip(n1, n2)]
    t = [_dot3(a, eye + b) for a, b in zip(t, n4)]
    bits = 3
    while (1 << bits) < c:
        pair = jnp.logical_and(same_block(bits + 1), jnp.logical_not(same_block(bits)))
        left = [_dot3(ti, jnp.where(pair, a, 0.0)) for ti, a in zip(t, a_list)]
        t = [ti - _dot3(li, ti) for ti, li in zip(t, left)]
        bits += 1
    return t


def _gdn_prep_kernel(alog_ref, dtb_ref, q_ref, k_ref, v_ref, ab_ref, w_ref, u0_ref, qe_ref, kw_ref, qk_ref, e_ref,
                     *, c, hb, n_heads, n_valid, lp):
    r = q_ref.shape[1]
    t = pl.program_id(1)
    h0 = pl.program_id(2) * hb
    bits = c.bit_length() - 1
    ri = lax.broadcasted_iota(jnp.int32, (r, r), 0)
    ci = lax.broadcasted_iota(jnp.int32, (r, r), 1)
    same = lax.shift_right_logical(ri, bits) == lax.shift_right_logical(ci, bits)
    eye = ri == ci
    lower = jnp.logical_and(ri >= ci, same)
    strict = jnp.logical_and(ri > ci, same)
    upper = jnp.logical_and(ri <= ci, same)
    lane = lax.broadcasted_iota(jnp.int32, (r, LANES), 1)
    row = lax.broadcasted_iota(jnp.int32, (r, 1), 0)
    ab = ab_ref[0]
    heads = range(hb)
    sls = [slice(hh * GDN_DK, (hh + 1) * GDN_DK) for hh in heads]
    ks, betas, a_mats = [], [], []
    for hh in heads:
        h = h0 + hh
        q = q_ref[0, :, sls[hh]]
        k = k_ref[0, :, sls[hh]]
        q = q * lax.rsqrt(jnp.sum(q * q, axis=-1, keepdims=True) + EPS) * (GDN_DK ** -0.5)
        k = k * lax.rsqrt(jnp.sum(k * k, axis=-1, keepdims=True) + EPS)
        a_col = jnp.sum(jnp.where(lane == h, ab, 0.0), axis=-1, keepdims=True)
        b_col = jnp.sum(jnp.where(lane == h + n_heads, ab, 0.0), axis=-1, keepdims=True)
        g_col = -jnp.exp(alog_ref[h]) * _softplus(a_col + dtb_ref[h])
        beta = jax.nn.sigmoid(b_col)
        if n_valid < lp:
            live = (row + t * r) < n_valid
            g_col = jnp.where(live, g_col, 0.0)
            beta = jnp.where(live, beta, 0.0)
        g_bc = jnp.broadcast_to(g_col, (r, r))
        g_row = jnp.sum(jnp.where(eye, g_bc, 0.0), axis=0, keepdims=True)
        gc_row = jnp.sum(jnp.where(upper, g_bc, 0.0), axis=0, keepdims=True)
        gc_col = jnp.sum(jnp.where(lower, jnp.broadcast_to(g_row, (r, r)), 0.0), axis=1, keepdims=True)
        decay = jnp.where(lower, jnp.exp(gc_col - gc_row), 0.0)
        eg = jnp.exp(gc_col)
        qe_ref[0, :, sls[hh]] = (q * eg).astype(qe_ref.dtype)
        qk = _dot_nt(q, k) * decay
        qkc = qk[:, 0:c]
        g_end = jnp.zeros((r, 1), F32)
        for j in range(r // c):
            if j:
                qkc = qkc + qk[:, j * c:(j + 1) * c]
            ge = gc_col[j * c + c - 1:(j + 1) * c, :]
            g_end = jnp.where(jnp.logical_and(row >= j * c, row < (j + 1) * c), ge, g_end)
            e_ref[0, j * SUBLANES:(j + 1) * SUBLANES, sls[hh]] = jnp.broadcast_to(jnp.exp(ge),
                                                                                 (SUBLANES, GDN_DV))
        qk_ref[0, :, hh * c:(hh + 1) * c] = qkc.astype(qk_ref.dtype)
        kw_ref[0, :, sls[hh]] = (k * jnp.exp(g_end - gc_col)).astype(kw_ref.dtype)
        a_mats.append(jnp.where(strict, _dot_nt(k, k) * beta * decay, 0.0))
        ks.append(k * (beta * eg))
        betas.append(beta)
    t_invs = _unit_lower_inverses(a_mats, r, c)
    ws = [_dot3(ti, kb) for ti, kb in zip(t_invs, ks)]
    u0s = [_dot3(ti, v_ref[0, :, sls[hh]] * betas[hh]) for hh, ti in zip(heads, t_invs)]
    for hh in heads:
        w_ref[0, :, sls[hh]] = ws[hh].astype(w_ref.dtype)
        u0_ref[0, :, sls[hh]] = u0s[hh]


def _gdn_scan_kernel(w_ref, u0_ref, qe_ref, kw_ref, qk_ref, e_ref, gate_ref, nw_ref, s0_ref, o_ref, s_ref,
                     *, c, hb):
    lp = w_ref.shape[1]
    for hh in range(hb):
        s_ref[0, hh] = s0_ref[0, hh]

    def chunk(t, carry):
        rows = pl.ds(pl.multiple_of(t * c, c), c)
        erow = pl.ds(pl.multiple_of(t * SUBLANES, SUBLANES), 1)
        heads = range(hb)
        sls = [slice(hh * GDN_DV, (hh + 1) * GDN_DV) for hh in heads]
        ss = [s_ref[0, hh] for hh in heads]
        us = [u0_ref[0, rows, sls[hh]] - _dot(w_ref[0, rows, sls[hh]], ss[hh]) for hh in heads]
        for hh in heads:
            s_ref[0, hh] = e_ref[0, erow, sls[hh]] * ss[hh] + _dot_tn(kw_ref[0, rows, sls[hh]], us[hh])
        os_ = [_dot(qe_ref[0, rows, sls[hh]], ss[hh]) + _dot(qk_ref[0, rows, hh * c:(hh + 1) * c], us[hh])
               for hh in heads]
        for hh in heads:
            o = os_[hh]
            o = o * lax.rsqrt(jnp.mean(o * o, axis=-1, keepdims=True) + EPS) * nw_ref[...]
            o_ref[0, rows, sls[hh]] = (o * _silu(gate_ref[0, rows, sls[hh]])).astype(o_ref.dtype)
        return carry

    lax.fori_loop(0, lp // c, chunk, 0)


def gated_deltanet(a_log, dt_bias, qkv, tail, norm_w, s0, *, n_valid):
    bsz, lp, _ = qkv.shape
    n_heads = s0.shape[1]
    c = min(GDN_CHUNK, lp)
    r = min(2 * GDN_CHUNK, lp)
    full = c == GDN_CHUNK
    hb1 = min(n_heads, 8) if full else n_heads
    hb2 = min(n_heads, 4) if full else n_heads
    nhb = n_heads // hb1
    hd = n_heads * GDN_DK
    ab_col = hd // LANES
    blk = lambda off: pl.BlockSpec((1, r, hb1 * GDN_DK), lambda i, t, h: (i, t, off + h))
    e_rows = (r // c) * SUBLANES
    w, u0, qe, kw, qk, e = pl.pallas_call(
        functools.partial(_gdn_prep_kernel, c=c, hb=hb1, n_heads=n_heads, n_valid=n_valid, lp=lp),
        grid=(bsz, lp // r, nhb),
        in_specs=[
            pl.BlockSpec(memory_space=pltpu.SMEM),
            pl.BlockSpec(memory_space=pltpu.SMEM),
            blk(0), blk(nhb), blk(2 * nhb),
            pl.BlockSpec((1, r, LANES), lambda i, t, h: (i, t, ab_col)),
        ],
        out_specs=[
            blk(0), blk(0), blk(0), blk(0),
            pl.BlockSpec((1, r, hb1 * c), lambda i, t, h: (i, t, h)),
            pl.BlockSpec((1, e_rows, hb1 * GDN_DV), lambda i, t, h: (i, t, h)),
        ],
        out_shape=[
            jax.ShapeDtypeStruct((bsz, lp, hd), MM_DTYPE),
            jax.ShapeDtypeStruct((bsz, lp, hd), F32),
            jax.ShapeDtypeStruct((bsz, lp, hd), MM_DTYPE),
            jax.ShapeDtypeStruct((bsz, lp, hd), MM_DTYPE),
            jax.ShapeDtypeStruct((bsz, lp, n_heads * c), MM_DTYPE),
            jax.ShapeDtypeStruct((bsz, (lp // c) * SUBLANES, hd), F32),
        ],
        compiler_params=_params("parallel", "parallel", "parallel"),
        name="gdn_prep",
    )(a_log, dt_bias, qkv, qkv, qkv, tail)
    blk2 = lambda width: pl.BlockSpec((1, lp, hb2 * width), lambda i, h: (i, 0, h))
    return pl.pallas_call(
        functools.partial(_gdn_scan_kernel, c=c, hb=hb2),
        grid=(bsz, n_heads // hb2),
        in_specs=[
            blk2(GDN_DK), blk2(GDN_DV), blk2(GDN_DK), blk2(GDN_DK), blk2(c),
            pl.BlockSpec((1, (lp // c) * SUBLANES, hb2 * GDN_DV), lambda i, h: (i, 0, h)),
            blk2(GDN_DV),
            pl.BlockSpec((1, GDN_DV), lambda i, h: (0, 0)),
            pl.BlockSpec((1, hb2, GDN_DK, GDN_DV), lambda i, h: (i, h, 0, 0)),
        ],
        out_specs=[
            blk2(GDN_DV),
            pl.BlockSpec((1, hb2, GDN_DK, GDN_DV), lambda i, h: (i, h, 0, 0)),
        ],
        out_shape=[
            jax.ShapeDtypeStruct((bsz, lp, n_heads * GDN_DV), MM_DTYPE),
            jax.ShapeDtypeStruct(s0.shape, F32),
        ],
        compiler_params=_params("parallel", "parallel"),
        name="gdn_scan",
    )(w, u0, qe, kw, qk, e, tail, norm_w.reshape(1, GDN_DV), s0)


def _ssd_kernel(x_ref, b_ref, c_ref, z_ref, dt_ref, alog_ref, dtb_ref, d_ref, nw_ref, s0_ref, y_ref, s_ref,
                y_scr, s_scr, *, hpg, lp, n_valid):
    g = pl.program_id(1)
    t = pl.program_id(2)
    c = x_ref.shape[1]
    p = SSD_HEAD_DIM

    @pl.when(t == 0)
    def _():
        s_scr[...] = s0_ref[0, 0].T

    ri = lax.broadcasted_iota(jnp.int32, (c, c), 0)
    ci = lax.broadcasted_iota(jnp.int32, (c, c), 1)
    lower = ri >= ci
    tri = lower.astype(F32)
    hl = lax.broadcasted_iota(jnp.int32, (LANES, LANES), 0)
    zl = lax.broadcasted_iota(jnp.int32, (LANES, LANES), 1)
    sel = jnp.logical_and(hl == g * hpg + zl, zl < hpg).astype(F32)

    xs = x_ref[0]
    bm = b_ref[0]
    cm = c_ref[0]
    dt = _softplus(dt_ref[0] + dtb_ref[...])
    if n_valid < lp:
        live = (lax.broadcasted_iota(jnp.int32, (c, 1), 0) + t * c) < n_valid
        dt = jnp.where(live, dt, 0.0)
    gcum = _dot_hi(tri, dt * (-jnp.exp(alog_ref[...])))
    gcg = _dot_hi(gcum, sel)
    gcg_t = gcg.T
    eh = lax.broadcasted_iota(jnp.int32, (LANES, hpg * p), 0)
    ec = lax.shift_right_logical(lax.broadcasted_iota(jnp.int32, (LANES, hpg * p), 1), p.bit_length() - 1)
    spread = (eh == g * hpg + ec).astype(MM_DTYPE)
    per_head = jnp.concatenate([dt, jnp.exp(gcum), jnp.exp(gcum[c - 1:c, :] - gcum),
                                jnp.broadcast_to(d_ref[...], (SUBLANES, LANES))], axis=0)
    hi, lo = _hi_lo(per_head)
    per_chan = (jnp.dot(hi, spread, preferred_element_type=F32)
                + jnp.dot(lo, spread, preferred_element_type=F32))
    dt_x = per_chan[0:c]
    eg_x = per_chan[c:2 * c]
    ew_x = per_chan[2 * c:3 * c]
    d_x = per_chan[3 * c:3 * c + 1]
    cb = _dot_nt(cm, bm)
    s = s_scr[...]
    xdt = xs * dt_x
    y_scr[...] = _dot(cm, s) * eg_x + d_x * xs
    for z in range(hpg):
        sl = slice(z * p, (z + 1) * p)
        decay = jnp.where(lower, jnp.exp(gcg[:, z:z + 1] - gcg_t[z:z + 1, :]), 0.0)
        y_scr[:, sl] = y_scr[:, sl] + _dot(cb * decay, xdt[:, sl])
    s_new = eg_x[c - 1:c, :] * s + _dot_tn(bm, xdt * ew_x)
    s_scr[...] = s_new

    @pl.when(t == pl.num_programs(2) - 1)
    def _():
        s_ref[0, 0] = s_new.T

    y = y_scr[...] * _silu(z_ref[0])
    y = y * lax.rsqrt(jnp.mean(y * y, axis=-1, keepdims=True) + EPS) * nw_ref[...]
    y_ref[0] = y.astype(y_ref.dtype)


def ssd_mixer(xbc, z_arr, dt_arr, a_log, dt_bias, d_skip, norm_w, s0_g, *, n_groups, z_col, dt_col, n_valid):
    bsz, lp, _ = xbc.shape
    n_heads = a_log.shape[0]
    assert n_heads == LANES
    hpg = n_heads // n_groups
    gw = hpg * SSD_HEAD_DIM
    d_inner = n_heads * SSD_HEAD_DIM
    c = min(SSD_CHUNK, lp)
    nc = lp // c
    assert gw % LANES == 0 and d_inner % gw == 0
    b_col0 = d_inner // SSD_STATE
    vec = lambda a: a.reshape(1, n_heads)
    return pl.pallas_call(
        functools.partial(_ssd_kernel, hpg=hpg, lp=lp, n_valid=n_valid),
        grid=(bsz, n_groups, nc),
        in_specs=[
            pl.BlockSpec((1, c, gw), lambda i, g, t: (i, t, g)),
            pl.BlockSpec((1, c, SSD_STATE), lambda i, g, t: (i, t, b_col0 + g)),
            pl.BlockSpec((1, c, SSD_STATE), lambda i, g, t: (i, t, b_col0 + n_groups + g)),
            pl.BlockSpec((1, c, gw), lambda i, g, t: (i, t, z_col + g)),
            pl.BlockSpec((1, c, LANES), lambda i, g, t: (i, t, dt_col)),
            pl.BlockSpec((1, LANES), lambda i, g, t: (0, 0)),
            pl.BlockSpec((1, LANES), lambda i, g, t: (0, 0)),
            pl.BlockSpec((1, LANES), lambda i, g, t: (0, 0)),
            pl.BlockSpec((1, gw), lambda i, g, t: (0, g)),
            pl.BlockSpec((1, 1, gw, SSD_STATE), lambda i, g, t: (i, g, 0, 0)),
        ],
        out_specs=[
            pl.BlockSpec((1, c, gw), lambda i, g, t: (i, t, g)),
            pl.BlockSpec((1, 1, gw, SSD_STATE), lambda i, g, t: (i, g, 0, 0)),
        ],
        out_shape=[
            jax.ShapeDtypeStruct((bsz, lp, d_inner), MM_DTYPE),
            jax.ShapeDtypeStruct(s0_g.shape, F32),
        ],
        scratch_shapes=[pltpu.VMEM((c, gw), F32), pltpu.VMEM((SSD_STATE, gw), F32)],
        compiler_params=_params("parallel", "parallel", "arbitrary"),
        name="ssd_mixer",
    )(xbc, xbc, xbc, z_arr, dt_arr, vec(a_log), vec(dt_bias), vec(d_skip), norm_w.reshape(1, d_inner), s0_g)


def _pad_rows(a, lp):
    return jnp.pad(a, ((0, 0), (0, lp - a.shape[1]), (0, 0)))


def _round_up(n, mult):
    return -(-n // mult) * mult


def _even_tail_weights(ev_w_in, o_ab, n_heads_b):
    o_g = o_ab + 2 * n_heads_b
    ab = ev_w_in[:, :, o_ab:o_g]
    gate = ev_w_in[:, :, o_g:]
    width = _round_up(gate.shape[2] + LANES, 3 * LANES)
    pad = jnp.zeros(ab.shape[:2] + (width - gate.shape[2] - 2 * n_heads_b,), ev_w_in.dtype)
    return jnp.concatenate([gate, ab, pad], axis=-1)


def _trunk(x3, mod_all, p, cache):
    bsz, l, d = x3.shape
    m = bsz * l
    x = x3.reshape(m, d)
    depth = p['norm_w'].shape[0]
    n_heads_a = p['swa_sinks'].shape[1]
    n_kv = p['n_kv']
    n_heads_b = p['gdn_a_log'].shape[1]
    n_heads_c = p['ssm_a_log'].shape[1]
    d_inner = n_heads_c * SSD_HEAD_DIM
    n_groups = (p['ssm_conv_w'].shape[2] - d_inner) // (2 * SSD_STATE)
    hpg = n_heads_c // n_groups
    qa = n_heads_a * SWA_HEAD_DIM
    kv = n_kv * SWA_HEAD_DIM
    conv_b = n_heads_b * (2 * GDN_DK + GDN_DV)
    conv_c = d_inner + 2 * n_groups * SSD_STATE
    expand = cache is not None
    out = {'k': [], 'v': [], 'gdn': [], 'gdn_conv': [], 'ssm': [], 'ssm_conv': []}

    def mod_of(layer, idx):
        v = mod_all[layer, idx]
        return v.reshape(1, m, d) if expand else v.reshape(bsz, 1, d)

    o_qkv = qa + 2 * kv
    o_ab = o_qkv + conv_b
    long_seq = l % SSD_CHUNK == 0 and l % GDN_CHUNK == 0
    lp = l if long_seq else _round_up(l, SUBLANES)
    for layer in range(depth):
        mo = functools.partial(mod_of, layer)
        h = ada_norm(x, p['norm_w'][layer, 0], mo(0), mo(1), l)
        act = swiglu_up(h, p['ffn_w_in'], 2 * layer)
        x = mm_residual([act], p['ffn_w_out'], 2 * layer, x, mo(2), 0.5, l, 512, 512)
        h = ada_norm(x, p['norm_w'][layer, 1], mo(3), mo(4), l)
        if layer % 2 == 0:
            e = layer // 2
            proj = mm(h, p['ev_w_in'], e, o_ab, 1024, 512)
            proj3 = proj.reshape(bsz, l, o_ab)
            tail3 = mm(h, p['ev_w_tail'], e, p['ev_w_tail'].shape[2], 1024, 256).reshape(bsz, l, -1)
            w_cols = jnp.concatenate([jnp.tile(p['swa_q_norm'][e], n_heads_a),
                                      jnp.tile(p['swa_k_norm'][e], n_kv)]).reshape(1, qa + kv)
            qk = head_norm(proj, w_cols, qa + kv).reshape(bsz, l, qa + kv)
            k_new = qk[:, :, qa:]
            v_new = proj3[:, :, qa + kv:qa + 2 * kv]
            if cache is None:
                o_a = swa_attention(p['swa_sinks'][e], qk, qk, qk, proj3, proj3, n_heads=n_heads_a, n_kv=n_kv,
                                    q_col=0, kc_col=qa // kv, kp_col=qa // kv, vc_col=(qa + kv) // kv,
                                    vp_col=(qa + kv) // kv, prev_is_cache=False)
                wb = min(WINDOW, l)
                out['k'].append(k_new[:, l - wb:].reshape(bsz, wb, n_kv, SWA_HEAD_DIM))
                out['v'].append(v_new[:, l - wb:].reshape(bsz, wb, n_kv, SWA_HEAD_DIM))
                conv0 = jnp.zeros((bsz, CONV_W - 1, conv_b), F32)
                s0 = jnp.zeros((bsz, n_heads_b, GDN_DK, GDN_DV), F32)
            else:
                kbuf = cache['k'][e].reshape(bsz, -1, kv)
                vbuf = cache['v'][e].reshape(bsz, -1, kv)
                o_a = swa_attention(p['swa_sinks'][e], _pad_rows(qk, lp), _pad_rows(qk, lp), kbuf,
                                    _pad_rows(v_new, lp), vbuf, n_heads=n_heads_a, n_kv=n_kv,
                                    q_col=0, kc_col=qa // kv, kp_col=0, vc_col=0, vp_col=0,
                                    prev_is_cache=True)[:, :l]
                out['k'].append(jnp.concatenate([kbuf, k_new], axis=1)[:, l:].reshape(bsz, -1, n_kv, SWA_HEAD_DIM))
                out['v'].append(jnp.concatenate([vbuf, v_new], axis=1)[:, l:].reshape(bsz, -1, n_kv, SWA_HEAD_DIM))
                conv0 = cache['gdn_conv'][e]
                s0 = cache['gdn'][e]
            if long_seq:
                out['gdn_conv'].append(proj3[:, l - (CONV_W - 1):, o_qkv:o_ab])
                qkv = conv_silu_long(proj3, o_qkv, conv_b, conv0, p['gdn_conv_w'][e], p['gdn_conv_b'][e])
            else:
                xp = jnp.concatenate([conv0, proj3[:, :, o_qkv:o_ab]], axis=1)
                out['gdn_conv'].append(xp[:, xp.shape[1] - (CONV_W - 1):])
                qkv = _pad_rows(conv_silu(xp, p['gdn_conv_w'][e], p['gdn_conv_b'][e]), lp)
                tail3 = _pad_rows(tail3, lp)
            o_b, s_new = gated_deltanet(p['gdn_a_log'][e], p['gdn_dt_bias'][e], qkv, tail3, p['gdn_norm_w'][e], s0,
                                        n_valid=l)
            out['gdn'].append(s_new)
            x = mm_residual([o_a.reshape(m, -1), o_b[:, :l].reshape(m, -1)], p['ev_w_out'], e, x, mo(5), 1.0, l,
                            1024, 512)
        else:
            oi = layer // 2
            o_dt = d_inner + conv_c
            proj3 = mm(h, p['od_w_in'], oi, o_dt, 1024, 512).reshape(bsz, l, o_dt)
            dt3 = mm(h, p['od_w_in'], oi, n_heads_c, 1024, LANES, col0=o_dt).reshape(bsz, l, n_heads_c)
            if cache is None:
                conv0 = jnp.zeros((bsz, CONV_W - 1, conv_c), F32)
                s0_g = jnp.zeros((bsz, n_groups, hpg * SSD_HEAD_DIM, SSD_STATE), F32)
            else:
                conv0 = cache['ssm_conv'][oi]
                s0_g = cache['ssm'][oi].reshape(bsz, n_groups, hpg * SSD_HEAD_DIM, SSD_STATE)
            if long_seq:
                out['ssm_conv'].append(proj3[:, l - (CONV_W - 1):, d_inner:o_dt])
                xbc = conv_silu_long(proj3, d_inner, conv_c, conv0, p['ssm_conv_w'][oi], p['ssm_conv_b'][oi])
                z3 = proj3
            else:
                xp = jnp.concatenate([conv0, proj3[:, :, d_inner:o_dt]], axis=1)
                out['ssm_conv'].append(xp[:, xp.shape[1] - (CONV_W - 1):])
                xbc = _pad_rows(conv_silu(xp, p['ssm_conv_w'][oi], p['ssm_conv_b'][oi]), lp)
                z3 = _pad_rows(proj3[:, :, :d_inner], lp)
                dt3 = _pad_rows(dt3, lp)
            y, s_g = ssd_mixer(xbc, z3, dt3, p['ssm_a_log'][oi], p['ssm_dt_bias'][oi], p['ssm_d'][oi],
                               p['ssm_norm_w'][oi], s0_g, n_groups=n_groups, z_col=0, dt_col=0, n_valid=l)
            out['ssm'].append(s_g.reshape(bsz, n_heads_c, SSD_HEAD_DIM, SSD_STATE))
            x = mm_residual([y[:, :l].reshape(m, d_inner)], p['od_w_out'], oi, x, mo(5), 1.0, l, 512, 512)
        h = ada_norm(x, p['norm_w'][layer, 2], mo(6), mo(7), l)
        act = swiglu_up(h, p['ffn_w_in'], 2 * layer + 1)
        x = mm_residual([act], p['ffn_w_out'], 2 * layer + 1, x, mo(8), 0.5, l, 512, 512)
    return (x.reshape(bsz, l, d), jnp.stack(out['k']), jnp.stack(out['v']), jnp.stack(out['gdn']),
            jnp.stack(out['gdn_conv']), jnp.stack(out['ssm']), jnp.stack(out['ssm_conv']))


def kernel(x_prompt, x_sample, c_prompt, c_sample, cache_swa_k, cache_swa_v, state_gdn, state_gdn_conv, state_ssm, state_ssm_conv, ada_w, ada_b, norm_w, ffn_w_in, ffn_w_out, ev_w_in, ev_w_out, swa_q_norm, swa_k_norm, swa_sinks, gdn_conv_w, gdn_conv_b, gdn_a_log, gdn_dt_bias, gdn_norm_w, od_w_in, od_w_out, ssm_conv_w, ssm_conv_b, ssm_a_log, ssm_dt_bias, ssm_d, ssm_norm_w):
    depth, d = norm_w.shape[0], norm_w.shape[2]
    n_kv = cache_swa_k.shape[3]
    n_heads_a = swa_sinks.shape[1]
    n_heads_b = gdn_a_log.shape[1]
    bp, bs = c_prompt.shape[0], c_sample.shape[0]
    r = bp + bs
    r_pad = -(-r // 16) * 16
    c_all = jnp.pad(jnp.concatenate([c_prompt, c_sample], axis=0), ((0, r_pad - r), (0, 0)))
    mod = ada_modulation(c_all, ada_w, ada_b).reshape(depth, r_pad, N_MOD, d).transpose(0, 2, 1, 3)
    mod_p = mod[:, :, :bp]
    mod_s = jnp.repeat(mod[:, :, bp:r], x_sample.shape[1], axis=2)
    o_ab = (n_heads_a + 2 * n_kv) * SWA_HEAD_DIM + n_heads_b * (2 * GDN_DK + GDN_DV)
    f2 = ffn_w_in.shape[3]
    p = {
        'norm_w': norm_w,
        'ffn_w_in': ffn_w_in.reshape(2 * depth, d, f2),
        'ffn_w_out': ffn_w_out.astype(MM_DTYPE).reshape(2 * depth, f2 // 2, d),
        'ev_w_in': ev_w_in, 'ev_w_tail': _even_tail_weights(ev_w_in, o_ab, n_heads_b),
        'ev_w_out': ev_w_out.astype(MM_DTYPE),
        'swa_q_norm': swa_q_norm, 'swa_k_norm': swa_k_norm, 'swa_sinks': swa_sinks, 'n_kv': n_kv,
        'gdn_conv_w': gdn_conv_w, 'gdn_conv_b': gdn_conv_b, 'gdn_a_log': gdn_a_log,
        'gdn_dt_bias': gdn_dt_bias, 'gdn_norm_w': gdn_norm_w,
        'od_w_in': od_w_in, 'od_w_out': od_w_out.astype(MM_DTYPE),
        'ssm_conv_w': ssm_conv_w, 'ssm_conv_b': ssm_conv_b, 'ssm_a_log': ssm_a_log,
        'ssm_dt_bias': ssm_dt_bias, 'ssm_d': ssm_d, 'ssm_norm_w': ssm_norm_w,
    }
    cache = {'k': cache_swa_k, 'v': cache_swa_v, 'gdn': state_gdn, 'gdn_conv': state_gdn_conv,
             'ssm': state_ssm, 'ssm_conv': state_ssm_conv}
    y_p, k_p, v_p, gdn_p, gdn_conv_p, ssm_p, ssm_conv_p = _trunk(x_prompt, mod_p, p, None)
    y_s, k_s, v_s, gdn_s, gdn_conv_s, ssm_s, ssm_conv_s = _trunk(x_sample, mod_s, p, cache)
    return (y_p, y_s, k_p, v_p, gdn_p, gdn_conv_p, ssm_p, ssm_conv_p,
            k_s, v_s, gdn_s, gdn_conv_s, ssm_s, ssm_conv_s)
```

```python
import functools
import math

import jax
import jax.numpy as jnp
from jax import lax
from jax.experimental import pallas as pl
from jax.experimental.pallas import tpu as pltpu

EPS = 1e-6
N_MOD = 9
SWA_HEAD_DIM = 64
WINDOW = 128
GDN_DK = 128
GDN_DV = 128
GDN_CHUNK = 64
CONV_W = 4
SSD_HEAD_DIM = 64
SSD_STATE = 128
SSD_CHUNK = 128

LANES = 128
SUBLANES = 8
VMEM_LIMIT_BYTES = 56 * 1024 * 1024

MM_DTYPE = jnp.bfloat16
HI = lax.Precision.HIGHEST
F32 = jnp.float32


def _dot(a, b):
    return jnp.dot(a.astype(MM_DTYPE), b.astype(MM_DTYPE), preferred_element_type=F32)


def _dot_nt(a, b):
    return lax.dot_general(a.astype(MM_DTYPE), b.astype(MM_DTYPE), (((1,), (1,)), ((), ())),
                           preferred_element_type=F32)


def _dot_tn(a, b):
    return lax.dot_general(a.astype(MM_DTYPE), b.astype(MM_DTYPE), (((0,), (0,)), ((), ())),
                           preferred_element_type=F32)


def _dot_hi(a, b):
    return jnp.dot(a, b, preferred_element_type=F32, precision=HI)


def _hi_lo(a):
    hi = a.astype(MM_DTYPE)
    return hi, (a - hi.astype(F32)).astype(MM_DTYPE)


def _dot3(a, b):
    ah, al = _hi_lo(a)
    bh, bl = _hi_lo(b)
    return (jnp.dot(ah, bh, preferred_element_type=F32)
            + (jnp.dot(ah, bl, preferred_element_type=F32) + jnp.dot(al, bh, preferred_element_type=F32)))


def _silu(x):
    return x * jax.nn.sigmoid(x)


def _softplus(x):
    return jnp.maximum(x, 0.0) + jnp.log1p(jnp.exp(-jnp.abs(x)))


def _params(*sem):
    return pltpu.CompilerParams(dimension_semantics=sem, vmem_limit_bytes=VMEM_LIMIT_BYTES)


def _tile(n, target):
    if n <= target:
        return n
    t = (target // LANES) * LANES
    while t > LANES and n % t:
        t -= LANES
    assert n % t == 0, (n, target)
    return t


def _ada_kernel(c_ref, w_ref, b_ref, o_ref):
    a = _silu(c_ref[...])
    o_ref[...] = _dot(a, w_ref[...]) + b_ref[...]


def ada_modulation(c_all, ada_w, ada_b):
    depth, d, n = ada_w.shape
    r = c_all.shape[0]
    bn = _tile(n, 512)
    return pl.pallas_call(
        _ada_kernel,
        grid=(depth, n // bn),
        in_specs=[
            pl.BlockSpec((r, d), lambda l, j: (0, 0)),
            pl.BlockSpec((None, d, bn), lambda l, j: (l, 0, j)),
            pl.BlockSpec((None, 1, bn), lambda l, j: (l, 0, j)),
        ],
        out_specs=pl.BlockSpec((None, r, bn), lambda l, j: (l, 0, j)),
        out_shape=jax.ShapeDtypeStruct((depth, r, n), F32),
        compiler_params=_params("parallel", "parallel"),
        name="ada_modulation",
    )(c_all, ada_w, ada_b.reshape(depth, 1, n))


def _ada_norm_kernel(x_ref, w_ref, shift_ref, scale_ref, o_ref):
    x = x_ref[...]
    h = x * lax.rsqrt(jnp.mean(x * x, axis=-1, keepdims=True) + EPS) * w_ref[...]
    o_ref[...] = (h * (1.0 + scale_ref[0]) + shift_ref[0]).astype(o_ref.dtype)


def _mod_spec(mod, rows_per_seq, bm, bn, two_d_grid):
    g, r, _ = mod.shape
    if r == 1:
        assert rows_per_seq % bm == 0
        per = rows_per_seq // bm
        if two_d_grid:
            return pl.BlockSpec((1, 1, bn), lambda i, j: (i // per, 0, j))
        return pl.BlockSpec((1, 1, bn), lambda i: (i // per, 0, 0))
    assert g == 1 and r == bm
    if two_d_grid:
        return pl.BlockSpec((1, r, bn), lambda i, j: (0, 0, j))
    return pl.BlockSpec((1, r, bn), lambda i: (0, 0, 0))


def _row_block(m, target, mod, rows_per_seq):
    return min(m, target, rows_per_seq) if mod.shape[1] == 1 else min(m, target)


def ada_norm(x, w, shift, scale, rows_per_seq):
    m, d = x.shape
    bm = _row_block(m, 256, shift, rows_per_seq)
    return pl.pallas_call(
        _ada_norm_kernel,
        grid=(m // bm,),
        in_specs=[
            pl.BlockSpec((bm, d), lambda i: (i, 0)),
            pl.BlockSpec((1, d), lambda i: (0, 0)),
            _mod_spec(shift, rows_per_seq, bm, d, False),
            _mod_spec(scale, rows_per_seq, bm, d, False),
        ],
        out_specs=pl.BlockSpec((bm, d), lambda i: (i, 0)),
        out_shape=jax.ShapeDtypeStruct((m, d), MM_DTYPE),
        compiler_params=_params("parallel"),
        name="ada_norm",
    )(x, w.reshape(1, d), shift, scale)


def _swiglu_kernel(x_ref, wg_ref, wu_ref, o_ref):
    x = x_ref[...]
    g = jnp.dot(x, wg_ref[...].astype(x.dtype), preferred_element_type=F32)
    u = jnp.dot(x, wu_ref[...].astype(x.dtype), preferred_element_type=F32)
    o_ref[...] = (_silu(g) * u).astype(o_ref.dtype)


def swiglu_up(h, w_in, idx):
    m, d = h.shape
    f = w_in.shape[2] // 2
    bm = min(m, 1024)
    bn = _tile(f, 256)
    nj = f // bn
    return pl.pallas_call(
        _swiglu_kernel,
        grid=(m // bm, nj),
        in_specs=[
            pl.BlockSpec((bm, d), lambda i, j: (i, 0)),
            pl.BlockSpec((None, d, bn), lambda i, j: (idx, 0, j)),
            pl.BlockSpec((None, d, bn), lambda i, j: (idx, 0, j + nj)),
        ],
        out_specs=pl.BlockSpec((bm, bn), lambda i, j: (i, j)),
        out_shape=jax.ShapeDtypeStruct((m, f), MM_DTYPE),
        compiler_params=_params("parallel", "arbitrary"),
        name="swiglu_up",
    )(h, w_in, w_in)


def _mm_resid_kernel(*refs, n_lhs, scale):
    a_refs, w_refs = refs[:n_lhs], refs[n_lhs:2 * n_lhs]
    x_ref, gate_ref, o_ref = refs[2 * n_lhs:]
    y = None
    for a_ref, w_ref in zip(a_refs, w_refs):
        a = a_ref[...]
        d = jnp.dot(a, w_ref[...].astype(a.dtype), preferred_element_type=F32)
        y = d if y is None else y + d
    o_ref[...] = x_ref[...] + (scale * gate_ref[0]) * y


def mm_residual(a_list, w, idx, x, gate, scale, rows_per_seq, bm, bn):
    m, k = a_list[0].shape
    n_lhs = len(a_list)
    n = w.shape[2]
    assert w.shape[1] == n_lhs * k
    bm = _row_block(m, bm, gate, rows_per_seq)
    bn = _tile(n, bn)
    w_spec = lambda part: pl.BlockSpec((None, k, bn), lambda i, j: (idx, part, j))
    return pl.pallas_call(
        functools.partial(_mm_resid_kernel, n_lhs=n_lhs, scale=scale),
        grid=(m // bm, n // bn),
        in_specs=(
            [pl.BlockSpec((bm, k), lambda i, j: (i, 0)) for _ in range(n_lhs)]
            + [w_spec(part) for part in range(n_lhs)]
            + [pl.BlockSpec((bm, bn), lambda i, j: (i, j)), _mod_spec(gate, rows_per_seq, bm, bn, True)]
        ),
        out_specs=pl.BlockSpec((bm, bn), lambda i, j: (i, j)),
        out_shape=jax.ShapeDtypeStruct((m, n), F32),
        compiler_params=_params("parallel", "arbitrary"),
        name="mm_residual",
    )(*a_list, *([w] * n_lhs), x, gate)


def _mm_kernel(x_ref, w_ref, o_ref):
    x = x_ref[...]
    o_ref[...] = jnp.dot(x, w_ref[...].astype(x.dtype), preferred_element_type=F32)


def mm(x, w, idx, n, bm, bn, col0=0):
    m, k = x.shape
    bm = min(m, bm)
    bn = _tile(n, bn)
    assert col0 % bn == 0
    j0 = col0 // bn
    return pl.pallas_call(
        _mm_kernel,
        grid=(m // bm, n // bn),
        in_specs=[
            pl.BlockSpec((bm, k), lambda i, j: (i, 0)),
            pl.BlockSpec((None, k, bn), lambda i, j: (idx, 0, j0 + j)),
        ],
        out_specs=pl.BlockSpec((bm, bn), lambda i, j: (i, j)),
        out_shape=jax.ShapeDtypeStruct((m, n), F32),
        compiler_params=_params("parallel", "arbitrary"),
        name="mm",
    )(x, w)


def _head_norm_kernel(x_ref, w_ref, avg_ref, o_ref):
    x = x_ref[...]
    ms = _dot_hi(x * x, avg_ref[...])
    o_ref[...] = x * lax.rsqrt(ms + EPS) * w_ref[...]


def head_norm(proj, w_cols, ncols):
    m = proj.shape[0]
    bm = min(m, 512)
    lane_head = jnp.arange(LANES) // SWA_HEAD_DIM
    avg = (lane_head[:, None] == lane_head[None, :]).astype(F32) / SWA_HEAD_DIM
    return pl.pallas_call(
        _head_norm_kernel,
        grid=(m // bm, ncols // LANES),
        in_specs=[
            pl.BlockSpec((bm, LANES), lambda i, j: (i, j)),
            pl.BlockSpec((1, LANES), lambda i, j: (0, j)),
            pl.BlockSpec((LANES, LANES), lambda i, j: (0, 0)),
        ],
        out_specs=pl.BlockSpec((bm, LANES), lambda i, j: (i, j)),
        out_shape=jax.ShapeDtypeStruct((m, ncols), F32),
        compiler_params=_params("parallel", "parallel"),
        name="head_norm",
    )(proj, w_cols, avg)


def _swa_kernel(sink_ref, q_ref, kc_ref, kp_ref, vc_ref, vp_ref, o_ref, *, n_heads, n_kv, always_prev):
    group = n_heads // n_kv
    w = WINDOW
    qr = q_ref.shape[1]
    qi_p = lax.broadcasted_iota(jnp.int32, (qr, w), 0)
    kj_p = lax.broadcasted_iota(jnp.int32, (qr, w), 1)
    qi_c = lax.broadcasted_iota(jnp.int32, (qr, qr), 0)
    kj_c = lax.broadcasted_iota(jnp.int32, (qr, qr), 1)
    has_prev = jnp.logical_or(pl.program_id(1) > 0, always_prev)
    valid_p = jnp.logical_and(kj_p > qi_p, has_prev)
    valid_c = kj_c <= qi_c
    dist_p = (w + qi_p - kj_p).astype(F32)
    dist_c = (qi_c - kj_c).astype(F32)
    sm_scale = SWA_HEAD_DIM ** -0.5
    neg_inf = jnp.float32(-jnp.inf)
    batch = min(group, 4)
    for kvh in range(n_kv):
        ks = slice(kvh * SWA_HEAD_DIM, (kvh + 1) * SWA_HEAD_DIM)
        kc = kc_ref[0, :, ks].astype(MM_DTYPE)
        kp = kp_ref[0, :, ks].astype(MM_DTYPE)
        vc = vc_ref[0, :, ks].astype(MM_DTYPE)
        vp = vp_ref[0, :, ks].astype(MM_DTYPE)
        for g0 in range(0, group, batch):
            hs_ = [kvh * group + g for g in range(g0, g0 + batch)]
            cols = [slice(h * SWA_HEAD_DIM, (h + 1) * SWA_HEAD_DIM) for h in hs_]
            slopes = [2.0 ** (-8.0 * (h + 1) / n_heads) for h in hs_]
            sinks = [sink_ref[h] for h in hs_]
            qs = [q_ref[0, :, c].astype(MM_DTYPE) for c in cols]
            s_p = [jnp.where(valid_p, _dot_nt(q, kp) * sm_scale - sl * dist_p, neg_inf) for q, sl in zip(qs, slopes)]
            s_c = [jnp.where(valid_c, _dot_nt(q, kc) * sm_scale - sl * dist_c, neg_inf) for q, sl in zip(qs, slopes)]
            ms = [jnp.maximum(jnp.maximum(jnp.max(a, axis=-1, keepdims=True), jnp.max(b, axis=-1, keepdims=True)), sk)
                  for a, b, sk in zip(s_p, s_c, sinks)]
            p_p = [jnp.exp(a - m) for a, m in zip(s_p, ms)]
            p_c = [jnp.exp(b - m) for b, m in zip(s_c, ms)]
            den = [jnp.sum(a, axis=-1, keepdims=True) + jnp.sum(b, axis=-1, keepdims=True) + jnp.exp(sk - m)
                   for a, b, sk, m in zip(p_p, p_c, sinks, ms)]
            os_ = [_dot(a, vp) + _dot(b, vc) for a, b in zip(p_p, p_c)]
            for c, o, d in zip(cols, os_, den):
                o_ref[0, :, c] = (o / d).astype(o_ref.dtype)


def swa_attention(sinks, q_arr, kc_arr, kp_arr, vc_arr, vp_arr, *, n_heads, n_kv, q_col, kc_col, kp_col,
                  vc_col, vp_col, prev_is_cache):
    b, l, _ = q_arr.shape
    qr = min(l, WINDOW)
    assert l % qr == 0 and (prev_is_cache or qr == WINDOW)
    nb = l // qr
    qw = n_heads * SWA_HEAD_DIM
    kw = n_kv * SWA_HEAD_DIM
    if prev_is_cache:
        assert nb == 1 and kp_arr.shape[1] == WINDOW
        prev_map = lambda col: (lambda i, n: (i, 0, col))
    else:
        prev_map = lambda col: (lambda i, n: (i, jnp.maximum(n - 1, 0), col))
    return pl.pallas_call(
        functools.partial(_swa_kernel, n_heads=n_heads, n_kv=n_kv, always_prev=prev_is_cache),
        grid=(b, nb),
        in_specs=[
            pl.BlockSpec(memory_space=pltpu.SMEM),
            pl.BlockSpec((1, qr, qw), lambda i, n: (i, n, q_col)),
            pl.BlockSpec((1, qr, kw), lambda i, n: (i, n, kc_col)),
            pl.BlockSpec((1, WINDOW, kw), prev_map(kp_col)),
            pl.BlockSpec((1, qr, kw), lambda i, n: (i, n, vc_col)),
            pl.BlockSpec((1, WINDOW, kw), prev_map(vp_col)),
        ],
        out_specs=pl.BlockSpec((1, qr, qw), lambda i, n: (i, n, 0)),
        out_shape=jax.ShapeDtypeStruct((b, l, qw), MM_DTYPE),
        compiler_params=_params("parallel", "arbitrary"),
        name="swa_attention",
    )(sinks, q_arr, kc_arr, kp_arr, vc_arr, vp_arr)


def _conv_silu_kernel(xp_ref, w_ref, b_ref, o_ref):
    l = o_ref.shape[1]
    y = sum(xp_ref[0, i:i + l, :] * w_ref[i:i + 1, :] for i in range(CONV_W)) + b_ref[...]
    o_ref[0] = _silu(y)


def conv_silu(xp, w, b):
    bsz, lp, c = xp.shape
    l = lp - (CONV_W - 1)
    bc = _tile(c, max(512, (512 * 2048) // lp // LANES * LANES))
    return pl.pallas_call(
        _conv_silu_kernel,
        grid=(bsz, c // bc),
        in_specs=[
            pl.BlockSpec((1, lp, bc), lambda i, j: (i, 0, j)),
            pl.BlockSpec((CONV_W, bc), lambda i, j: (0, j)),
            pl.BlockSpec((1, bc), lambda i, j: (0, j)),
        ],
        out_specs=pl.BlockSpec((1, l, bc), lambda i, j: (i, 0, j)),
        out_shape=jax.ShapeDtypeStruct((bsz, l, c), F32),
        compiler_params=_params("parallel", "parallel"),
        name="conv_silu",
    )(xp, w, b.reshape(1, c))


def _conv_silu_long_kernel(x_ref, buf_ref, w_ref, b_ref, o_ref, head_scr):
    x = x_ref[0]
    acc = x * w_ref[CONV_W - 1:CONV_W, :] + b_ref[...]
    for s in range(1, CONV_W):
        acc = acc + pltpu.roll(x, s, 0) * w_ref[CONV_W - 1 - s:CONV_W - s, :]
    o_ref[0] = _silu(acc)
    lo = SUBLANES - (CONV_W - 1)
    head_scr[lo:SUBLANES, :] = buf_ref[0]
    head_scr[SUBLANES:2 * SUBLANES, :] = x[0:SUBLANES]
    yh = sum(head_scr[lo + i:lo + i + SUBLANES, :] * w_ref[i:i + 1, :] for i in range(CONV_W)) + b_ref[...]
    o_ref[0, 0:SUBLANES, :] = _silu(yh)


def conv_silu_long(x_arr, col0, c, buf, w, b):
    bsz, l, _ = x_arr.shape
    bc = _tile(c, 512)
    assert col0 % bc == 0 and l % SUBLANES == 0
    j0 = col0 // bc
    return pl.pallas_call(
        _conv_silu_long_kernel,
        grid=(bsz, c // bc),
        in_specs=[
            pl.BlockSpec((1, l, bc), lambda i, j: (i, 0, j0 + j)),
            pl.BlockSpec((1, CONV_W - 1, bc), lambda i, j: (i, 0, j)),
            pl.BlockSpec((CONV_W, bc), lambda i, j: (0, j)),
            pl.BlockSpec((1, bc), lambda i, j: (0, j)),
        ],
        out_specs=pl.BlockSpec((1, l, bc), lambda i, j: (i, 0, j)),
        out_shape=jax.ShapeDtypeStruct((bsz, l, c), F32),
        scratch_shapes=[pltpu.VMEM((2 * SUBLANES, bc), F32)],
        compiler_params=_params("parallel", "parallel"),
        name="conv_silu_long",
    )(x_arr, buf, w, b.reshape(1, c))


def _unit_lower_inverses(a_list, r, c):
    ri = lax.broadcasted_iota(jnp.int32, (r, r), 0)
    ci = lax.broadcasted_iota(jnp.int32, (r, r), 1)
    eye = (ri == ci).astype(F32)

    def same_block(bits):
        return lax.shift_right_logical(ri, bits) == lax.shift_right_logical(ci, bits)

    n1 = [jnp.where(same_block(3), a, 0.0) for a in a_list]
    n2 = [_dot3(n, n) for n in n1]
    n4 = [_dot3(n, n) for n in n2]
    t = [_dot3(eye - a, eye + b) for a, b in zip(n1, n2)]
    t = [_dot3(a, eye + b) for a, b in zip(t, n4)]
    bits = 3
    while (1 << bits) < c:
        pair = jnp.logical_and(same_block(bits + 1), jnp.logical_not(same_block(bits)))
        left = [_dot3(ti, jnp.where(pair, a, 0.0)) for ti, a in zip(t, a_list)]
        t = [ti - _dot3(li, ti) for ti, li in zip(t, left)]
        bits += 1
    return t


def _gdn_prep_kernel(alog_ref, dtb_ref, q_ref, k_ref, v_ref, ab_ref, w_ref, u0_ref, qe_ref, kw_ref, qk_ref, e_ref,
                     *, c, hb, n_heads, n_valid, lp):
    r = q_ref.shape[1]
    t = pl.program_id(1)
    h0 = pl.program_id(2) * hb
    bits = c.bit_length() - 1
    ri = lax.broadcasted_iota(jnp.int32, (r, r), 0)
    ci = lax.broadcasted_iota(jnp.int32, (r, r), 1)
    same = lax.shift_right_logical(ri, bits) == lax.shift_right_logical(ci, bits)
    eye = ri == ci
    lower = jnp.logical_and(ri >= ci, same)
    strict = jnp.logical_and(ri > ci, same)
    upper = jnp.logical_and(ri <= ci, same)
    lane = lax.broadcasted_iota(jnp.int32, (r, LANES), 1)
    row = lax.broadcasted_iota(jnp.int32, (r, 1), 0)
    ab = ab_ref[0]
    heads = range(hb)
    sls = [slice(hh * GDN_DK, (hh + 1) * GDN_DK) for hh in heads]
    ks, betas, a_mats = [], [], []
    for hh in heads:
        h = h0 + hh
        q = q_ref[0, :, sls[hh]]
        k = k_ref[0, :, sls[hh]]
        q = q * lax.rsqrt(jnp.sum(q * q, axis=-1, keepdims=True) + EPS) * (GDN_DK ** -0.5)
        k = k * lax.rsqrt(jnp.sum(k * k, axis=-1, keepdims=True) + EPS)
        a_col = jnp.sum(jnp.where(lane == h, ab, 0.0), axis=-1, keepdims=True)
        b_col = jnp.sum(jnp.where(lane == h + n_heads, ab, 0.0), axis=-1, keepdims=True)
        g_col = -jnp.exp(alog_ref[h]) * _softplus(a_col + dtb_ref[h])
        beta = jax.nn.sigmoid(b_col)
        if n_valid < lp:
            live = (row + t * r) < n_valid
            g_col = jnp.where(live, g_col, 0.0)
            beta = jnp.where(live, beta, 0.0)
        g_bc = jnp.broadcast_to(g_col, (r, r))
        g_row = jnp.sum(jnp.where(eye, g_bc, 0.0), axis=0, keepdims=True)
        gc_row = jnp.sum(jnp.where(upper, g_bc, 0.0), axis=0, keepdims=True)
        gc_col = jnp.sum(jnp.where(lower, jnp.broadcast_to(g_row, (r, r)), 0.0), axis=1, keepdims=True)
        decay = jnp.where(lower, jnp.exp(gc_col - gc_row), 0.0)
        eg = jnp.exp(gc_col)
        qe_ref[0, :, sls[hh]] = (q * eg).astype(qe_ref.dtype)
        qk = _dot_nt(q, k) * decay
        qkc = qk[:, 0:c]
        g_end = jnp.zeros((r, 1), F32)
        for j in range(r // c):
            if j:
                qkc = qkc + qk[:, j * c:(j + 1) * c]
            ge = gc_col[j * c + c - 1:(j + 1) * c, :]
            g_end = jnp.where(jnp.logical_and(row >= j * c, row < (j + 1) * c), ge, g_end)
            e_ref[0, j * SUBLANES:(j + 1) * SUBLANES, sls[hh]] = jnp.broadcast_to(jnp.exp(ge),
                                                                                 (SUBLANES, GDN_DV))
        qk_ref[0, :, hh * c:(hh + 1) * c] = qkc.astype(qk_ref.dtype)
        kw_ref[0, :, sls[hh]] = (k * jnp.exp(g_end - gc_col)).astype(kw_ref.dtype)
        a_mats.append(jnp.where(strict, _dot_nt(k, k) * beta * decay, 0.0))
        ks.append(k * (beta * eg))
        betas.append(beta)
    t_invs = _unit_lower_inverses(a_mats, r, c)
    ws = [_dot3(ti, kb) for ti, kb in zip(t_invs, ks)]
    u0s = [_dot3(ti, v_ref[0, :, sls[hh]] * betas[hh]) for hh, ti in zip(heads, t_invs)]
    for hh in heads:
        w_ref[0, :, sls[hh]] = ws[hh].astype(w_ref.dtype)
        u0_ref[0, :, sls[hh]] = u0s[hh]


def _gdn_scan_kernel(w_ref, u0_ref, qe_ref, kw_ref, qk_ref, e_ref, gate_ref, nw_ref, s0_ref, o_ref, s_ref,
                     *, c, hb):
    lp = w_ref.shape[1]
    for hh in range(hb):
        s_ref[0, hh] = s0_ref[0, hh]

    def chunk(t, carry):
        rows = pl.ds(pl.multiple_of(t * c, c), c)
        erow = pl.ds(pl.multiple_of(t * SUBLANES, SUBLANES), 1)
        heads = range(hb)
        sls = [slice(hh * GDN_DV, (hh + 1) * GDN_DV) for hh in heads]
        ss = [s_ref[0, hh] for hh in heads]
        us = [u0_ref[0, rows, sls[hh]] - _dot(w_ref[0, rows, sls[hh]], ss[hh]) for hh in heads]
        for hh in heads:
            s_ref[0, hh] = e_ref[0, erow, sls[hh]] * ss[hh] + _dot_tn(kw_ref[0, rows, sls[hh]], us[hh])
        os_ = [_dot(qe_ref[0, rows, sls[hh]], ss[hh]) + _dot(qk_ref[0, rows, hh * c:(hh + 1) * c], us[hh])
               for hh in heads]
        for hh in heads:
            o = os_[hh]
            o = o * lax.rsqrt(jnp.mean(o * o, axis=-1, keepdims=True) + EPS) * nw_ref[...]
            o_ref[0, rows, sls[hh]] = (o * _silu(gate_ref[0, rows, sls[hh]])).astype(o_ref.dtype)
        return carry

    lax.fori_loop(0, lp // c, chunk, 0)


def gated_deltanet(a_log, dt_bias, qkv, tail, norm_w, s0, *, n_valid):
    bsz, lp, _ = qkv.shape
    n_heads = s0.shape[1]
    c = min(GDN_CHUNK, lp)
    r = min(2 * GDN_CHUNK, lp)
    full = c == GDN_CHUNK
    hb1 = min(n_heads, 8) if full else n_heads
    hb2 = min(n_heads, 4) if full else n_heads
    nhb = n_heads // hb1
    hd = n_heads * GDN_DK
    ab_col = hd // LANES
    blk = lambda off: pl.BlockSpec((1, r, hb1 * GDN_DK), lambda i, t, h: (i, t, off + h))
    e_rows = (r // c) * SUBLANES
    w, u0, qe, kw, qk, e = pl.pallas_call(
        functools.partial(_gdn_prep_kernel, c=c, hb=hb1, n_heads=n_heads, n_valid=n_valid, lp=lp),
        grid=(bsz, lp // r, nhb),
        in_specs=[
            pl.BlockSpec(memory_space=pltpu.SMEM),
            pl.BlockSpec(memory_space=pltpu.SMEM),
            blk(0), blk(nhb), blk(2 * nhb),
            pl.BlockSpec((1, r, LANES), lambda i, t, h: (i, t, ab_col)),
        ],
        out_specs=[
            blk(0), blk(0), blk(0), blk(0),
            pl.BlockSpec((1, r, hb1 * c), lambda i, t, h: (i, t, h)),
            pl.BlockSpec((1, e_rows, hb1 * GDN_DV), lambda i, t, h: (i, t, h)),
        ],
        out_shape=[
            jax.ShapeDtypeStruct((bsz, lp, hd), MM_DTYPE),
            jax.ShapeDtypeStruct((bsz, lp, hd), F32),
            jax.ShapeDtypeStruct((bsz, lp, hd), MM_DTYPE),
            jax.ShapeDtypeStruct((bsz, lp, hd), MM_DTYPE),
            jax.ShapeDtypeStruct((bsz, lp, n_heads * c), MM_DTYPE),
            jax.ShapeDtypeStruct((bsz, (lp // c) * SUBLANES, hd), F32),
        ],
        compiler_params=_params("parallel", "parallel", "parallel"),
        name="gdn_prep",
    )(a_log, dt_bias, qkv, qkv, qkv, tail)
    blk2 = lambda width: pl.BlockSpec((1, lp, hb2 * width), lambda i, h: (i, 0, h))
    return pl.pallas_call(
        functools.partial(_gdn_scan_kernel, c=c, hb=hb2),
        grid=(bsz, n_heads // hb2),
        in_specs=[
            blk2(GDN_DK), blk2(GDN_DV), blk2(GDN_DK), blk2(GDN_DK), blk2(c),
            pl.BlockSpec((1, (lp // c) * SUBLANES, hb2 * GDN_DV), lambda i, h: (i, 0, h)),
            blk2(GDN_DV),
            pl.BlockSpec((1, GDN_DV), lambda i, h: (0, 0)),
            pl.BlockSpec((1, hb2, GDN_DK, GDN_DV), lambda i, h: (i, h, 0, 0)),
        ],
        out_specs=[
            blk2(GDN_DV),
            pl.BlockSpec((1, hb2, GDN_DK, GDN_DV), lambda i, h: (i, h, 0, 0)),
        ],
        out_shape=[
            jax.ShapeDtypeStruct((bsz, lp, n_heads * GDN_DV), MM_DTYPE),
            jax.ShapeDtypeStruct(s0.shape, F32),
        ],
        compiler_params=_params("parallel", "parallel"),
        name="gdn_scan",
    )(w, u0, qe, kw, qk, e, tail, norm_w.reshape(1, GDN_DV), s0)


def _ssd_kernel(x_ref, b_ref, c_ref, z_ref, dt_ref, alog_ref, dtb_ref, d_ref, nw_ref, s0_ref, y_ref, s_ref,
                y_scr, s_scr, *, hpg, lp, n_valid):
    g = pl.program_id(1)
    t = pl.program_id(2)
    c = x_ref.shape[1]
    p = SSD_HEAD_DIM

    @pl.when(t == 0)
    def _():
        s_scr[...] = s0_ref[0, 0].T

    ri = lax.broadcasted_iota(jnp.int32, (c, c), 0)
    ci = lax.broadcasted_iota(jnp.int32, (c, c), 1)
    lower = ri >= ci
    tri = lower.astype(F32)
    hl = lax.broadcasted_iota(jnp.int32, (LANES, LANES), 0)
    zl = lax.broadcasted_iota(jnp.int32, (LANES, LANES), 1)
    sel = jnp.logical_and(hl == g * hpg + zl, zl < hpg).astype(F32)

    xs = x_ref[0]
    bm = b_ref[0]
    cm = c_ref[0]
    dt = _softplus(dt_ref[0] + dtb_ref[...])
    if n_valid < lp:
        live = (lax.broadcasted_iota(jnp.int32, (c, 1), 0) + t * c) < n_valid
        dt = jnp.where(live, dt, 0.0)
    gcum = _dot_hi(tri, dt * (-jnp.exp(alog_ref[...])))
    gcg = _dot_hi(gcum, sel)
    gcg_t = gcg.T
    eh = lax.broadcasted_iota(jnp.int32, (LANES, hpg * p), 0)
    ec = lax.shift_right_logical(lax.broadcasted_iota(jnp.int32, (LANES, hpg * p), 1), p.bit_length() - 1)
    spread = (eh == g * hpg + ec).astype(MM_DTYPE)
    per_head = jnp.concatenate([dt, jnp.exp(gcum), jnp.exp(gcum[c - 1:c, :] - gcum),
                                jnp.broadcast_to(d_ref[...], (SUBLANES, LANES))], axis=0)
    hi, lo = _hi_lo(per_head)
    per_chan = (jnp.dot(hi, spread, preferred_element_type=F32)
                + jnp.dot(lo, spread, preferred_element_type=F32))
    dt_x = per_chan[0:c]
    eg_x = per_chan[c:2 * c]
    ew_x = per_chan[2 * c:3 * c]
    d_x = per_chan[3 * c:3 * c + 1]
    cb = _dot_nt(cm, bm)
    s = s_scr[...]
    xdt = xs * dt_x
    y_scr[...] = _dot(cm, s) * eg_x + d_x * xs
    for z in range(hpg):
        sl = slice(z * p, (z + 1) * p)
        decay = jnp.where(lower, jnp.exp(gcg[:, z:z + 1] - gcg_t[z:z + 1, :]), 0.0)
        y_scr[:, sl] = y_scr[:, sl] + _dot(cb * decay, xdt[:, sl])
    s_new = eg_x[c - 1:c, :] * s + _dot_tn(bm, xdt * ew_x)
    s_scr[...] = s_new

    @pl.when(t == pl.num_programs(2) - 1)
    def _():
        s_ref[0, 0] = s_new.T

    y = y_scr[...] * _silu(z_ref[0])
    y = y * lax.rsqrt(jnp.mean(y * y, axis=-1, keepdims=True) + EPS) * nw_ref[...]
    y_ref[0] = y.astype(y_ref.dtype)


def ssd_mixer(xbc, z_arr, dt_arr, a_log, dt_bias, d_skip, norm_w, s0_g, layer, *, n_groups, z_col, dt_col,
              n_valid):
    bsz, lp, _ = xbc.shape
    n_heads = a_log.shape[0]
    assert n_heads == LANES
    hpg = n_heads // n_groups
    gw = hpg * SSD_HEAD_DIM
    d_inner = n_heads * SSD_HEAD_DIM
    c = min(SSD_CHUNK, lp)
    nc = lp // c
    assert gw % LANES == 0 and d_inner % gw == 0
    b_col0 = d_inner // SSD_STATE
    vec = lambda a: a.reshape(1, n_heads)
    return pl.pallas_call(
        functools.partial(_ssd_kernel, hpg=hpg, lp=lp, n_valid=n_valid),
        grid=(bsz, n_groups, nc),
        in_specs=[
            pl.BlockSpec((1, c, gw), lambda i, g, t: (i, t, g)),
            pl.BlockSpec((1, c, SSD_STATE), lambda i, g, t: (i, t, b_col0 + g)),
            pl.BlockSpec((1, c, SSD_STATE), lambda i, g, t: (i, t, b_col0 + n_groups + g)),
            pl.BlockSpec((1, c, gw), lambda i, g, t: (i, t, z_col + g)),
            pl.BlockSpec((1, c, LANES), lambda i, g, t: (i, t, dt_col)),
            pl.BlockSpec((1, LANES), lambda i, g, t: (0, 0)),
            pl.BlockSpec((1, LANES), lambda i, g, t: (0, 0)),
            pl.BlockSpec((1, LANES), lambda i, g, t: (0, 0)),
            pl.BlockSpec((1, gw), lambda i, g, t: (0, g)),
            pl.BlockSpec((None, 1, 1, gw, SSD_STATE), lambda i, g, t: (layer, i, g, 0, 0)),
        ],
        out_specs=[
            pl.BlockSpec((1, c, gw), lambda i, g, t: (i, t, g)),
            pl.BlockSpec((1, 1, gw, SSD_STATE), lambda i, g, t: (i, g, 0, 0)),
        ],
        out_shape=[
            jax.ShapeDtypeStruct((bsz, lp, d_inner), MM_DTYPE),
            jax.ShapeDtypeStruct(s0_g.shape[1:], F32),
        ],
        scratch_shapes=[pltpu.VMEM((c, gw), F32), pltpu.VMEM((SSD_STATE, gw), F32)],
        compiler_params=_params("parallel", "parallel", "arbitrary"),
        name="ssd_mixer",
    )(xbc, xbc, xbc, z_arr, dt_arr, vec(a_log), vec(dt_bias), vec(d_skip), norm_w.reshape(1, d_inner), s0_g)


def _pad_rows(a, lp):
    return jnp.pad(a, ((0, 0), (0, lp - a.shape[1]), (0, 0)))


def _round_up(n, mult):
    return -(-n // mult) * mult


def _even_tail_weights(ev_w_in, o_ab, n_heads_b):
    o_g = o_ab + 2 * n_heads_b
    ab = ev_w_in[:, :, o_ab:o_g]
    gate = ev_w_in[:, :, o_g:]
    width = _round_up(gate.shape[2] + LANES, 3 * LANES)
    pad = jnp.zeros(ab.shape[:2] + (width - gate.shape[2] - 2 * n_heads_b,), ev_w_in.dtype)
    return jnp.concatenate([gate, ab, pad], axis=-1)


def _trunk(x3, mod_all, p, cache):
    bsz, l, d = x3.shape
    m = bsz * l
    x = x3.reshape(m, d)
    depth = p['norm_w'].shape[0]
    n_heads_a = p['swa_sinks'].shape[1]
    n_kv = p['n_kv']
    n_heads_b = p['gdn_a_log'].shape[1]
    n_heads_c = p['ssm_a_log'].shape[1]
    d_inner = n_heads_c * SSD_HEAD_DIM
    n_groups = (p['ssm_conv_w'].shape[2] - d_inner) // (2 * SSD_STATE)
    hpg = n_heads_c // n_groups
    qa = n_heads_a * SWA_HEAD_DIM
    kv = n_kv * SWA_HEAD_DIM
    conv_b = n_heads_b * (2 * GDN_DK + GDN_DV)
    conv_c = d_inner + 2 * n_groups * SSD_STATE
    expand = cache is not None
    out = {'k': [], 'v': [], 'gdn': [], 'gdn_conv': [], 'ssm': [], 'ssm_conv': []}

    def mod_of(layer, idx):
        v = mod_all[layer, idx]
        return v.reshape(1, m, d) if expand else v.reshape(bsz, 1, d)

    o_qkv = qa + 2 * kv
    o_ab = o_qkv + conv_b
    long_seq = l % SSD_CHUNK == 0 and l % GDN_CHUNK == 0
    lp = l if long_seq else _round_up(l, SUBLANES)
    for layer in range(depth):
        mo = functools.partial(mod_of, layer)
        h = ada_norm(x, p['norm_w'][layer, 0], mo(0), mo(1), l)
        act = swiglu_up(h, p['ffn_w_in'], 2 * layer)
        x = mm_residual([act], p['ffn_w_out'], 2 * layer, x, mo(2), 0.5, l, 512, 512)
        h = ada_norm(x, p['norm_w'][layer, 1], mo(3), mo(4), l)
        if layer % 2 == 0:
            e = layer // 2
            proj = mm(h, p['ev_w_in'], e, o_ab, 1024, 512)
            proj3 = proj.reshape(bsz, l, o_ab)
            tail3 = mm(h, p['ev_w_tail'], e, p['ev_w_tail'].shape[2], 1024, 256).reshape(bsz, l, -1)
            w_cols = jnp.concatenate([jnp.tile(p['swa_q_norm'][e], n_heads_a),
                                      jnp.tile(p['swa_k_norm'][e], n_kv)]).reshape(1, qa + kv)
            qk = head_norm(proj, w_cols, qa + kv).reshape(bsz, l, qa + kv)
            k_new = qk[:, :, qa:]
            v_new = proj3[:, :, qa + kv:qa + 2 * kv]
            if cache is None:
                o_a = swa_attention(p['swa_sinks'][e], qk, qk, qk, proj3, proj3, n_heads=n_heads_a, n_kv=n_kv,
                                    q_col=0, kc_col=qa // kv, kp_col=qa // kv, vc_col=(qa + kv) // kv,
                                    vp_col=(qa + kv) // kv, prev_is_cache=False)
                wb = min(WINDOW, l)
                out['k'].append(k_new[:, l - wb:].reshape(bsz, wb, n_kv, SWA_HEAD_DIM))
                out['v'].append(v_new[:, l - wb:].reshape(bsz, wb, n_kv, SWA_HEAD_DIM))
                conv0 = jnp.zeros((bsz, CONV_W - 1, conv_b), F32)
                s0 = jnp.zeros((bsz, n_heads_b, GDN_DK, GDN_DV), F32)
            else:
                kbuf = cache['k'][e].reshape(bsz, -1, kv)
                vbuf = cache['v'][e].reshape(bsz, -1, kv)
                o_a = swa_attention(p['swa_sinks'][e], _pad_rows(qk, lp), _pad_rows(qk, lp), kbuf,
                                    _pad_rows(v_new, lp), vbuf, n_heads=n_heads_a, n_kv=n_kv,
                                    q_col=0, kc_col=qa // kv, kp_col=0, vc_col=0, vp_col=0,
                                    prev_is_cache=True)[:, :l]
                out['k'].append(jnp.concatenate([kbuf, k_new], axis=1)[:, l:].reshape(bsz, -1, n_kv, SWA_HEAD_DIM))
                out['v'].append(jnp.concatenate([vbuf, v_new], axis=1)[:, l:].reshape(bsz, -1, n_kv, SWA_HEAD_DIM))
                conv0 = cache['gdn_conv'][e]
                s0 = cache['gdn'][e]
            if long_seq:
                out['gdn_conv'].append(proj3[:, l - (CONV_W - 1):, o_qkv:o_ab])
                qkv = conv_silu_long(proj3, o_qkv, conv_b, conv0, p['gdn_conv_w'][e], p['gdn_conv_b'][e])
            else:
                xp = jnp.concatenate([conv0, proj3[:, :, o_qkv:o_ab]], axis=1)
                out['gdn_conv'].append(xp[:, xp.shape[1] - (CONV_W - 1):])
                qkv = _pad_rows(conv_silu(xp, p['gdn_conv_w'][e], p['gdn_conv_b'][e]), lp)
                tail3 = _pad_rows(tail3, lp)
            o_b, s_new = gated_deltanet(p['gdn_a_log'][e], p['gdn_dt_bias'][e], qkv, tail3, p['gdn_norm_w'][e], s0,
                                        n_valid=l)
            out['gdn'].append(s_new)
            x = mm_residual([o_a.reshape(m, -1), o_b[:, :l].reshape(m, -1)], p['ev_w_out'], e, x, mo(5), 1.0, l,
                            1024, 512)
        else:
            oi = layer // 2
            o_dt = d_inner + conv_c
            proj3 = mm(h, p['od_w_in'], oi, o_dt, 1024, 512).reshape(bsz, l, o_dt)
            dt3 = mm(h, p['od_w_in'], oi, n_heads_c, 1024, LANES, col0=o_dt).reshape(bsz, l, n_heads_c)
            if cache is None:
                conv0 = jnp.zeros((bsz, CONV_W - 1, conv_c), F32)
                s0_g, s0_layer = jnp.zeros((1, bsz, n_groups, hpg * SSD_HEAD_DIM, SSD_STATE), F32), 0
            else:
                conv0 = cache['ssm_conv'][oi]
                s0_g, s0_layer = cache['ssm'].reshape(-1, bsz, n_groups, hpg * SSD_HEAD_DIM, SSD_STATE), oi
            if long_seq:
                out['ssm_conv'].append(proj3[:, l - (CONV_W - 1):, d_inner:o_dt])
                xbc = conv_silu_long(proj3, d_inner, conv_c, conv0, p['ssm_conv_w'][oi], p['ssm_conv_b'][oi])
                z3 = proj3
            else:
                xp = jnp.concatenate([conv0, proj3[:, :, d_inner:o_dt]], axis=1)
                out['ssm_conv'].append(xp[:, xp.shape[1] - (CONV_W - 1):])
                xbc = _pad_rows(conv_silu(xp, p['ssm_conv_w'][oi], p['ssm_conv_b'][oi]), lp)
                z3 = _pad_rows(proj3[:, :, :d_inner], lp)
                dt3 = _pad_rows(dt3, lp)
            y, s_g = ssd_mixer(xbc, z3, dt3, p['ssm_a_log'][oi], p['ssm_dt_bias'][oi], p['ssm_d'][oi],
                               p['ssm_norm_w'][oi], s0_g, s0_layer, n_groups=n_groups, z_col=0, dt_col=0,
                               n_valid=l)
            out['ssm'].append(s_g.reshape(bsz, n_heads_c, SSD_HEAD_DIM, SSD_STATE))
            x = mm_residual([y[:, :l].reshape(m, d_inner)], p['od_w_out'], oi, x, mo(5), 1.0, l, 512, 512)
        h = ada_norm(x, p['norm_w'][layer, 2], mo(6), mo(7), l)
        act = swiglu_up(h, p['ffn_w_in'], 2 * layer + 1)
        x = mm_residual([act], p['ffn_w_out'], 2 * layer + 1, x, mo(8), 0.5, l, 512, 512)
    return (x.reshape(bsz, l, d), jnp.stack(out['k']), jnp.stack(out['v']), jnp.stack(out['gdn']),
            jnp.stack(out['gdn_conv']), jnp.stack(out['ssm']), jnp.stack(out['ssm_conv']))


def kernel(x_prompt, x_sample, c_prompt, c_sample, cache_swa_k, cache_swa_v, state_gdn, state_gdn_conv, state_ssm, state_ssm_conv, ada_w, ada_b, norm_w, ffn_w_in, ffn_w_out, ev_w_in, ev_w_out, swa_q_norm, swa_k_norm, swa_sinks, gdn_conv_w, gdn_conv_b, gdn_a_log, gdn_dt_bias, gdn_norm_w, od_w_in, od_w_out, ssm_conv_w, ssm_conv_b, ssm_a_log, ssm_dt_bias, ssm_d, ssm_norm_w):
    depth, d = norm_w.shape[0], norm_w.shape[2]
    n_kv = cache_swa_k.shape[3]
    n_heads_a = swa_sinks.shape[1]
    n_heads_b = gdn_a_log.shape[1]
    bp, bs = c_prompt.shape[0], c_sample.shape[0]
    r = bp + bs
    r_pad = -(-r // 16) * 16
    c_all = jnp.pad(jnp.concatenate([c_prompt, c_sample], axis=0), ((0, r_pad - r), (0, 0)))
    mod = ada_modulation(c_all, ada_w, ada_b).reshape(depth, r_pad, N_MOD, d).transpose(0, 2, 1, 3)
    mod_p = mod[:, :, :bp]
    mod_s = jnp.repeat(mod[:, :, bp:r], x_sample.shape[1], axis=2)
    o_ab = (n_heads_a + 2 * n_kv) * SWA_HEAD_DIM + n_heads_b * (2 * GDN_DK + GDN_DV)
    f2 = ffn_w_in.shape[3]
    p = {
        'norm_w': norm_w,
        'ffn_w_in': ffn_w_in.reshape(2 * depth, d, f2),
        'ffn_w_out': ffn_w_out.astype(MM_DTYPE).reshape(2 * depth, f2 // 2, d),
        'ev_w_in': ev_w_in, 'ev_w_tail': _even_tail_weights(ev_w_in, o_ab, n_heads_b),
        'ev_w_out': ev_w_out.astype(MM_DTYPE),
        'swa_q_norm': swa_q_norm, 'swa_k_norm': swa_k_norm, 'swa_sinks': swa_sinks, 'n_kv': n_kv,
        'gdn_conv_w': gdn_conv_w, 'gdn_conv_b': gdn_conv_b, 'gdn_a_log': gdn_a_log,
        'gdn_dt_bias': gdn_dt_bias, 'gdn_norm_w': gdn_norm_w,
        'od_w_in': od_w_in, 'od_w_out': od_w_out.astype(MM_DTYPE),
        'ssm_conv_w': ssm_conv_w, 'ssm_conv_b': ssm_conv_b, 'ssm_a_log': ssm_a_log,
        'ssm_dt_bias': ssm_dt_bias, 'ssm_d': ssm_d, 'ssm_norm_w': ssm_norm_w,
    }
    cache = {'k': cache_swa_k, 'v': cache_swa_v, 'gdn': state_gdn, 'gdn_conv': state_gdn_conv,
             'ssm': state_ssm, 'ssm_conv': state_ssm_conv}
    y_p, k_p, v_p, gdn_p, gdn_conv_p, ssm_p, ssm_conv_p = _trunk(x_prompt, mod_p, p, None)
    y_s, k_s, v_s, gdn_s, gdn_conv_s, ssm_s, ssm_conv_s = _trunk(x_sample, mod_s, p, cache)
    return (y_p, y_s, k_p, v_p, gdn_p, gdn_conv_p, ssm_p, ssm_conv_p,
            k_s, v_s, gdn_s, gdn_conv_s, ssm_s, ssm_conv_s)
```
